```python
import jax, jax.numpy as jnp
from jax import lax
import numpy as np

D_MODEL = 1024
BATCH = 2
SEQ = 8192
DEPTH = 2

CHUNK = 64
SB_BLOCK = 128
EPS = 1e-6
GLA_HEADS = 4
GLA_HEAD_K = 64
GLA_HEAD_V = 128
GLA_DK = GLA_HEADS * GLA_HEAD_K
GLA_DV = GLA_HEADS * GLA_HEAD_V
GLA_GATE_RANK = 16
GLA_GATE_NORMALIZER = 16.0
SB_HEADS = 8
SB_HEAD_DIM = 64
SB_D = SB_HEADS * SB_HEAD_DIM
MIX_WIDTH = GLA_DV + SB_D
IN_WIDTH = 2 * GLA_DK + 2 * GLA_DV + GLA_GATE_RANK + 3 * SB_D
CONV_WIDTH = 31
D_FF = -(-8 * D_MODEL // (3 * 256)) * 256
N_EVEN = (DEPTH + 1) // 2
N_ODD = DEPTH // 2

kernel_name = "hybrid_gla_stickbreaking_conformer_trunk"


def _rms_f32(xf, g):
    return xf * lax.rsqrt(jnp.mean(xf * xf, axis=-1, keepdims=True) + EPS) * g.astype(jnp.float32)


def rms_norm(x, g):
    return _rms_f32(x.astype(jnp.float32), g).astype(x.dtype)


def gla_mixer(q, k, v, r, gate_lr, w_gate2, b_gate, g_out):
    f32 = jnp.float32
    B, T, _ = q.shape
    nc = T // CHUNK
    log_a = jax.nn.log_sigmoid((gate_lr @ w_gate2 + b_gate).astype(f32)) / GLA_GATE_NORMALIZER

    def split(t, hd):
        return t.astype(f32).reshape(B, nc, CHUNK, -1, hd).transpose(0, 3, 1, 2, 4)

    qc = split(q, GLA_HEAD_K) * (GLA_HEAD_K ** -0.5)
    kc = split(k, GLA_HEAD_K)
    vc = split(v, GLA_HEAD_V)
    bc = jnp.cumsum(split(log_a, GLA_HEAD_K), axis=3)
    b_end = bc[:, :, :, -1:, :]
    k_end = kc * jnp.exp(b_end - bc)
    scores = jnp.einsum('bhnik,bhnjk->bhnij', qc, k_end)
    intra = jnp.einsum('bhnij,bhnjv->bhniv', scores, vc)
    chunk_update = jnp.einsum('bhnjk,bhnjv->bhnkv', k_end, vc)
    chunk_decay = jnp.exp(b_end[:, :, :, 0, :])

    def step(s, inp):
        a, u = inp
        return a[..., None] * s + u, s

    s0 = jnp.zeros((B, GLA_HEADS, GLA_HEAD_K, GLA_HEAD_V), f32)
    _, s_prev = lax.scan(step, s0, (jnp.moveaxis(chunk_decay, 2, 0), jnp.moveaxis(chunk_update, 2, 0)))
    s_prev = jnp.moveaxis(s_prev, 0, 2)
    inter = jnp.einsum('bhnik,bhnkv->bhniv', qc * jnp.exp(b_end), s_prev)
    o = _rms_f32(intra + inter, g_out)
    o = o.transpose(0, 2, 3, 1, 4).reshape(B, T, GLA_DV)
    return o * jax.nn.silu(r.astype(f32))


def stick_breaking_mixer(q, k, v, g_q, g_k):
    f32 = jnp.float32
    B, T, _ = q.shape

    def heads(t):
        return t.astype(f32).reshape(B, T, SB_HEADS, SB_HEAD_DIM).transpose(0, 2, 1, 3)

    qh = _rms_f32(heads(q), g_q) * (SB_HEAD_DIM ** -0.5)
    kh = _rms_f32(heads(k), g_k)
    vh = heads(v)
    outs = []
    for blk in range(T // SB_BLOCK):
        q0 = blk * SB_BLOCK
        kv_len = q0 + SB_BLOCK
        past = np.arange(kv_len)[None, :] < np.arange(q0, kv_len)[:, None]
        z = jnp.einsum('bhtd,bhsd->bhts', qh[:, :, q0:kv_len], kh[:, :, :kv_len])
        log_keep = jnp.where(past, jax.nn.log_sigmoid(-z), 0.0)
        between = lax.cumsum(log_keep, axis=3, reverse=True) - log_keep
        w = jnp.where(past, jnp.exp(jax.nn.log_sigmoid(z) + between), 0.0)
        outs.append(jnp.einsum('bhts,bhsd->bhtd', w, vh[:, :, :kv_len]))
    o = jnp.concatenate(outs, axis=2)
    return o.transpose(0, 2, 1, 3).reshape(B, T, SB_D)


def hybrid_mixer(h, w_in, w_gate2, b_gate, g_gla, g_q, g_k, w_out):
    proj = h @ w_in
    cuts = np.cumsum([GLA_DK, GLA_DK, GLA_DV, GLA_DV, GLA_GATE_RANK, SB_D, SB_D])
    gq, gk, gv, gr, glr, sq, sk, sv = jnp.split(proj, cuts, axis=-1)
    o_gla = gla_mixer(gq, gk, gv, gr, glr, w_gate2, b_gate, g_gla)
    o_sb = stick_breaking_mixer(sq, sk, sv, g_q, g_k)
    o = jnp.concatenate([o_gla, o_sb], axis=-1).astype(h.dtype)
    return o @ w_out


def conformer_conv(h, w_pw1, b_pw1, w_dw, b_dw, ln_g, ln_b, w_pw2, b_pw2):
    a = h @ w_pw1 + b_pw1
    u = a[..., :D_MODEL] * jax.nn.sigmoid(a[..., D_MODEL:])
    u = lax.conv_general_dilated(u, w_dw[:, None, :].astype(u.dtype), window_strides=(1,),
                                 padding=[(CONV_WIDTH - 1, 0)],
                                 dimension_numbers=('NWC', 'WIO', 'NWC'),
                                 feature_group_count=D_MODEL) + b_dw
    uf = u.astype(jnp.float32)
    mu = jnp.mean(uf, axis=-1, keepdims=True)
    var = jnp.mean(jnp.square(uf - mu), axis=-1, keepdims=True)
    u = ((uf - mu) * lax.rsqrt(var + EPS) * ln_g + ln_b).astype(h.dtype)
    return jax.nn.silu(u) @ w_pw2 + b_pw2


def swiglu(h, wg, wu, wd):
    return (jax.nn.silu(h @ wg) * (h @ wu)) @ wd


def setup_inputs(seed: int = 0) -> dict:
    key = jax.random.key(seed)
    ks = iter(jax.random.split(key, 32))

    def nrm(shape, scale):
        return jax.random.normal(next(ks), shape, jnp.float32) * scale

    def gain(shape):
        return 1.0 + nrm(shape, 0.05)

    return {
        "x": nrm((BATCH, SEQ, D_MODEL), 1.0),
        "mix_norm": gain((DEPTH, D_MODEL)),
        "ffn_norm": gain((DEPTH, D_MODEL)),
        "hy_w_in": nrm((N_EVEN, D_MODEL, IN_WIDTH), D_MODEL ** -0.5),
        "hy_w_gate2": nrm((N_EVEN, GLA_GATE_RANK, GLA_DK), GLA_GATE_RANK ** -0.5),
        "hy_b_gate": nrm((N_EVEN, GLA_DK), 0.1),
        "hy_gla_norm": gain((N_EVEN, GLA_HEAD_V)),
        "hy_sb_q_norm": gain((N_EVEN, SB_HEAD_DIM)),
        "hy_sb_k_norm": gain((N_EVEN, SB_HEAD_DIM)),
        "hy_w_out": nrm((N_EVEN, MIX_WIDTH, D_MODEL), MIX_WIDTH ** -0.5),
        "cv_w_pw1": nrm((N_ODD, D_MODEL, 2 * D_MODEL), D_MODEL ** -0.5),
        "cv_b_pw1": nrm((N_ODD, 2 * D_MODEL), 0.02),
        "cv_w_dw": nrm((N_ODD, CONV_WIDTH, D_MODEL), CONV_WIDTH ** -0.5),
        "cv_b_dw": nrm((N_ODD, D_MODEL), 0.02),
        "cv_ln_g": gain((N_ODD, D_MODEL)),
        "cv_ln_b": nrm((N_ODD, D_MODEL), 0.02),
        "cv_w_pw2": nrm((N_ODD, D_MODEL, D_MODEL), D_MODEL ** -0.5),
        "cv_b_pw2": nrm((N_ODD, D_MODEL), 0.02),
        "ffn_w_gate": nrm((DEPTH, D_MODEL, D_FF), D_MODEL ** -0.5),
        "ffn_w_up": nrm((DEPTH, D_MODEL, D_FF), D_MODEL ** -0.5),
        "ffn_w_down": nrm((DEPTH, D_FF, D_MODEL), D_FF ** -0.5),
    }


def reference(x, mix_norm, ffn_norm, hy_w_in, hy_w_gate2, hy_b_gate, hy_gla_norm,
              hy_sb_q_norm, hy_sb_k_norm, hy_w_out, cv_w_pw1, cv_b_pw1, cv_w_dw, cv_b_dw,
              cv_ln_g, cv_ln_b, cv_w_pw2, cv_b_pw2, ffn_w_gate, ffn_w_up, ffn_w_down):
    h = x
    for layer in range(DEPTH):
        hn = rms_norm(h, mix_norm[layer])
        if layer % 2 == 0:
            i = layer // 2
            mix = hybrid_mixer(hn, hy_w_in[i], hy_w_gate2[i], hy_b_gate[i], hy_gla_norm[i],
                               hy_sb_q_norm[i], hy_sb_k_norm[i], hy_w_out[i])
        else:
            i = layer // 2
            mix = conformer_conv(hn, cv_w_pw1[i], cv_b_pw1[i], cv_w_dw[i], cv_b_dw[i],
                                 cv_ln_g[i], cv_ln_b[i], cv_w_pw2[i], cv_b_pw2[i])
        h = h + mix.astype(h.dtype)
        h = h + swiglu(rms_norm(h, ffn_norm[layer]), ffn_w_gate[layer], ffn_w_up[layer],
                       ffn_w_down[layer]).astype(h.dtype)
    return h
```

```python
import functools

import numpy as np
import jax
import jax.numpy as jnp
from jax import lax
from jax.experimental import pallas as pl
from jax.experimental.pallas import tpu as pltpu

F32 = jnp.float32
BF16 = jnp.bfloat16

D_MODEL = 1024
CHUNK = 64
EPS = 1e-6
GLA_HEADS = 4
GLA_HEAD_K = 64
GLA_HEAD_V = 128
GLA_DK = GLA_HEADS * GLA_HEAD_K
GLA_DV = GLA_HEADS * GLA_HEAD_V
GLA_GATE_RANK = 16
GLA_GATE_NORMALIZER = 16.0
SB_HEADS = 8
SB_HEAD_DIM = 64
SB_D = SB_HEADS * SB_HEAD_DIM
CONV_WIDTH = 31
D_FF = 2816

LANES = 128
MIB = 1024 * 1024
PROJ_W = GLA_DV + GLA_DV + 3 * SB_D + GLA_DK
COL_GV, COL_GR, COL_SQ, COL_SK, COL_SV, COL_GQ = 0, 512, 1024, 1536, 2048, 2560

NT_DIMS = (((1,), (1,)), ((), ()))


def _cparams(sem, vmem_mib):
    return pltpu.CompilerParams(dimension_semantics=sem, vmem_limit_bytes=vmem_mib * MIB)


def _rms(x, g):
    return x * lax.rsqrt(jnp.mean(x * x, axis=-1, keepdims=True) + EPS) * g


def _sigmoid(x):
    return 1.0 / (1.0 + jnp.exp(-x))


def _log_sigmoid(x):
    return jnp.minimum(x, 0.0) - jnp.log(1.0 + jnp.exp(-jnp.abs(x)))


def _split_bf16(x):
    hi = x.astype(BF16)
    lo = (x - hi.astype(F32)).astype(BF16)
    return hi, lo


def _const_spec(shape):
    nd = len(shape)
    return pl.BlockSpec(shape, lambda *_: (0,) * nd, pipeline_mode=pl.Buffered(1))


def _inproj_kernel(x_ref, g_ref, w_ref, wkt_ref, wgt_ref, proj_ref, kt_ref, glrt_ref):
    xb = _rms(x_ref[...], g_ref[...]).astype(BF16)
    proj_ref[...] = jnp.dot(xb, w_ref[...], preferred_element_type=F32)
    kt_ref[...] = lax.dot_general(wkt_ref[...], xb, NT_DIMS, preferred_element_type=F32)
    glrt_ref[...] = lax.dot_general(wgt_ref[...], xb, NT_DIMS, preferred_element_type=F32)


def _inproj(x2d, g, w, wkt, wgt, tm=512):
    m = x2d.shape[0]
    return pl.pallas_call(
        _inproj_kernel,
        grid=(m // tm,),
        in_specs=[
            pl.BlockSpec((tm, D_MODEL), lambda i: (i, 0)),
            _const_spec((1, D_MODEL)),
            _const_spec((D_MODEL, PROJ_W)),
            _const_spec((GLA_DK, D_MODEL)),
            _const_spec((LANES, D_MODEL)),
        ],
        out_specs=[
            pl.BlockSpec((tm, PROJ_W), lambda i: (i, 0)),
            pl.BlockSpec((GLA_DK, tm), lambda i: (0, i)),
            pl.BlockSpec((LANES, tm), lambda i: (0, i)),
        ],
        out_shape=[
            jax.ShapeDtypeStruct((m, PROJ_W), F32),
            jax.ShapeDtypeStruct((GLA_DK, m), F32),
            jax.ShapeDtypeStruct((LANES, m), F32),
        ],
        compiler_params=_cparams(("parallel",), 48),
        name="inproj",
    )(x2d, g, w, wkt, wgt)


GLA_TILE = 2 * CHUNK


def _gla_cumsum_matrix():
    r = np.arange(GLA_TILE)[:, None]
    i = np.arange(GLA_TILE)[None, :]
    same = (r // CHUNK) == (i // CHUNK)
    rem = (r > i) & same
    tot0 = np.broadcast_to(r < CHUNK, (GLA_TILE, GLA_TILE))
    tot1 = np.broadcast_to(r >= CHUNK, (GLA_TILE, GLA_TILE))
    return np.concatenate([rem, tot0, tot1], axis=1).astype(np.float32)


def _gla_kernel(q_ref, v_ref, r_ref, kt_ref, glrt_ref, w2t_ref, bg_ref, cm_ref, gout_ref,
                o_ref, s_ref, *, tiles):
    @pl.when(pl.program_id(1) == 0)
    def _():
        s_ref[...] = jnp.zeros_like(s_ref)

    lane = lax.broadcasted_iota(jnp.int32, (GLA_DK, GLA_TILE), 1)
    for t in range(tiles):
        cols = slice(t * GLA_TILE, (t + 1) * GLA_TILE)
        glr = glrt_ref[:, cols].astype(BF16)
        pre = jnp.dot(w2t_ref[...], glr, preferred_element_type=F32) + bg_ref[...]
        la = _log_sigmoid(pre) * (1.0 / GLA_GATE_NORMALIZER)
        hi, lo = _split_bf16(la)
        sums = (jnp.dot(hi, cm_ref[...], preferred_element_type=F32)
                + jnp.dot(lo, cm_ref[...], preferred_element_type=F32))
        k_end = (kt_ref[:, cols] * jnp.exp(sums[:, :GLA_TILE])).astype(BF16)
        decay = (jnp.exp(sums[:, GLA_TILE:2 * GLA_TILE]), jnp.exp(sums[:, 2 * GLA_TILE:]))
        k_half = (jnp.where(lane < CHUNK, k_end, jnp.zeros_like(k_end)),
                  jnp.where(lane >= CHUNK, k_end, jnp.zeros_like(k_end)))
        v = v_ref[cols, :].astype(BF16)
        for j in range(2):
            rows = slice(t * GLA_TILE + j * CHUNK, t * GLA_TILE + (j + 1) * CHUNK)
            for h in range(GLA_HEADS):
                kr = slice(h * GLA_HEAD_K, (h + 1) * GLA_HEAD_K)
                vc = slice(h * GLA_HEAD_V, (h + 1) * GLA_HEAD_V)
                upd = jnp.dot(k_half[j][kr, :], v[:, vc], preferred_element_type=F32)
                s_ref[kr, vc] = decay[j][kr, :] * s_ref[kr, vc] + upd
            q = q_ref[rows, :].astype(BF16)
            o = jnp.dot(q, s_ref[...].astype(BF16), preferred_element_type=F32)
            o = o * (GLA_HEAD_K ** -0.5)
            gate = r_ref[rows, :]
            for h in range(GLA_HEADS):
                vc = slice(h * GLA_HEAD_V, (h + 1) * GLA_HEAD_V)
                oh = _rms(o[:, vc], gout_ref[...])
                g = gate[:, vc]
                o_ref[rows, vc] = (oh * (g * _sigmoid(g))).astype(o_ref.dtype)


def _gla(proj, kt, glrt, w2t, bg, cm, gout, batch, seq, tr=256):
    nt = seq // tr
    m = batch * seq
    kernel = functools.partial(_gla_kernel, tiles=tr // GLA_TILE)
    return pl.pallas_call(
        kernel,
        grid=(batch, nt),
        in_specs=[
            pl.BlockSpec((tr, GLA_DK), lambda b, i: (b * nt + i, COL_GQ // GLA_DK)),
            pl.BlockSpec((tr, GLA_DV), lambda b, i: (b * nt + i, COL_GV // GLA_DV)),
            pl.BlockSpec((tr, GLA_DV), lambda b, i: (b * nt + i, COL_GR // GLA_DV)),
            pl.BlockSpec((GLA_DK, tr), lambda b, i: (0, b * nt + i)),
            pl.BlockSpec((LANES, tr), lambda b, i: (0, b * nt + i)),
            _const_spec((GLA_DK, LANES)),
            _const_spec((GLA_DK, LANES)),
            _const_spec((GLA_TILE, 3 * GLA_TILE)),
            _const_spec((1, GLA_HEAD_V)),
        ],
        out_specs=pl.BlockSpec((tr, GLA_DV), lambda b, i: (b * nt + i, 0)),
        out_shape=jax.ShapeDtypeStruct((m, GLA_DV), BF16),
        scratch_shapes=[pltpu.VMEM((GLA_DK, GLA_DV), F32)],
        compiler_params=_cparams(("arbitrary", "arbitrary"), 32),
        name="gla",
    )(proj, proj, proj, kt, glrt, w2t, bg, cm, gout)


def _sb_prep_kernel(q_ref, k_ref, v_ref, gq_ref, gk_ref, qn_ref, kn_ref, vb_ref):
    tm = q_ref.shape[0]
    lane = lax.broadcasted_iota(jnp.int32, (tm, LANES), 1)
    first = lane < SB_HEAD_DIM

    def head_norm(x, g):
        x2 = x * x
        s0 = jnp.sum(jnp.where(first, x2, 0.0), axis=-1, keepdims=True)
        s1 = jnp.sum(jnp.where(first, 0.0, x2), axis=-1, keepdims=True)
        ms = jnp.where(first, s0, s1) * (1.0 / SB_HEAD_DIM)
        return x * lax.rsqrt(ms + EPS) * g

    for p in range(SB_D // LANES):
        cols = slice(p * LANES, (p + 1) * LANES)
        qn = head_norm(q_ref[:, cols], gq_ref[...]) * (SB_HEAD_DIM ** -0.5)
        qn_ref[:, cols] = qn.astype(BF16)
        kn_ref[:, cols] = head_norm(k_ref[:, cols], gk_ref[...]).astype(BF16)
    vb_ref[...] = v_ref[...].astype(BF16)


def _sb_prep(proj, gq2, gk2, tm=512):
    m = proj.shape[0]
    out = jax.ShapeDtypeStruct((m, SB_D), BF16)
    return pl.pallas_call(
        _sb_prep_kernel,
        grid=(m // tm,),
        in_specs=[
            pl.BlockSpec((tm, SB_D), lambda i: (i, COL_SQ // SB_D)),
            pl.BlockSpec((tm, SB_D), lambda i: (i, COL_SK // SB_D)),
            pl.BlockSpec((tm, SB_D), lambda i: (i, COL_SV // SB_D)),
            _const_spec((1, LANES)),
            _const_spec((1, LANES)),
        ],
        out_specs=[pl.BlockSpec((tm, SB_D), lambda i: (i, 0))] * 3,
        out_shape=[out, out, out],
        compiler_params=_cparams(("parallel",), 32),
        name="sb_prep",
    )(proj, proj, proj, gq2, gk2)


SB_TQ = 256


def _sb_kernel(q_ref, k_ref, v_ref, o_ref, acc_ref, c_ref):
    qi = pl.program_id(2)
    tq = SB_TQ
    lane = lax.broadcasted_iota(jnp.int32, (tq, LANES), 1)
    q = q_ref[...]
    zero = jnp.zeros_like(q)
    q_heads = (jnp.where(lane < SB_HEAD_DIM, q, zero), jnp.where(lane >= SB_HEAD_DIM, q, zero))
    row = lax.broadcasted_iota(jnp.int32, (tq, tq), 0)
    col = lax.broadcasted_iota(jnp.int32, (tq, tq), 1)
    later = jnp.where(row > col, 1.0, 0.0).astype(BF16)
    past = col < row

    acc_ref[...] = jnp.zeros_like(acc_ref)
    c_ref[...] = jnp.zeros_like(c_ref)

    def block(kb, masked):
        start = pl.multiple_of(kb * tq, tq)
        kblk = k_ref[pl.ds(start, tq), :]
        vblk = v_ref[pl.ds(start, tq), :]
        for h in range(2):
            z = lax.dot_general(q_heads[h], kblk, NT_DIMS, preferred_element_type=F32)
            soft = jnp.log(1.0 + jnp.exp(-jnp.abs(z)))
            log_keep = -(jnp.maximum(z, 0.0) + soft)
            log_beta = jnp.minimum(z, 0.0) - soft
            if masked:
                log_keep = jnp.where(past, log_keep, 0.0)
            hi, lo = _split_bf16(log_keep)
            between = (jnp.dot(hi, later, preferred_element_type=F32)
                       + jnp.dot(lo, later, preferred_element_type=F32))
            c = c_ref[h]
            w = jnp.exp(log_beta + between + jnp.concatenate([c, c], axis=1))
            if masked:
                w = jnp.where(past, w, 0.0)
            acc_ref[h] += jnp.dot(w.astype(BF16), vblk, preferred_element_type=F32)
            c_ref[h] = c + jnp.sum(log_keep, axis=-1, keepdims=True)

    block(qi, True)

    def body(i, carry):
        block(qi - 1 - i, False)
        return carry

    lax.fori_loop(0, qi, body, 0)
    o_ref[...] = jnp.where(lane < SB_HEAD_DIM, acc_ref[0], acc_ref[1]).astype(o_ref.dtype)


def _sb_attention(qn, kn, vb, batch, seq):
    nq = seq // SB_TQ
    pairs = SB_D // LANES
    q3, k3, v3 = (a.reshape(batch, seq, SB_D) for a in (qn, kn, vb))
    out = pl.pallas_call(
        _sb_kernel,
        grid=(batch, pairs, nq),
        in_specs=[
            pl.BlockSpec((None, SB_TQ, LANES), lambda b, p, i: (b, i, p)),
            pl.BlockSpec((None, seq, LANES), lambda b, p, i: (b, 0, p)),
            pl.BlockSpec((None, seq, LANES), lambda b, p, i: (b, 0, p)),
        ],
        out_specs=pl.BlockSpec((None, SB_TQ, LANES), lambda b, p, i: (b, i, p)),
        out_shape=jax.ShapeDtypeStruct((batch, seq, SB_D), BF16),
        scratch_shapes=[pltpu.VMEM((2, SB_TQ, LANES), F32), pltpu.VMEM((2, SB_TQ, LANES), F32)],
        compiler_params=_cparams(("parallel", "parallel", "arbitrary"), 32),
        name="sb_attention",
    )(q3, k3, v3)
    return out.reshape(batch * seq, SB_D)


FF_CHUNK = 256


def _ffn_tail(h1, g_ref, wg_ref, wu_ref, wd_ref, out_ref, acc_ref):
    hb = _rms(h1, g_ref[...]).astype(BF16)
    acc_ref[...] = h1
    for c in range(D_FF // FF_CHUNK):
        cols = slice(c * FF_CHUNK, (c + 1) * FF_CHUNK)
        a = jnp.dot(hb, wg_ref[:, cols], preferred_element_type=F32)
        u = jnp.dot(hb, wu_ref[:, cols], preferred_element_type=F32)
        act = (a * _sigmoid(a) * u).astype(BF16)
        acc_ref[...] += jnp.dot(act, wd_ref[cols, :], preferred_element_type=F32)
    out_ref[...] = acc_ref[...]


def _ffn_kernel(h_ref, g_ref, wg_ref, wu_ref, wd_ref, out_ref, acc_ref):
    _ffn_tail(h_ref[...], g_ref, wg_ref, wu_ref, wd_ref, out_ref, acc_ref)


def _outproj_ffn_kernel(h_ref, og_ref, os_ref, wo_ref, g_ref, wg_ref, wu_ref, wd_ref,
                        out_ref, acc_ref):
    mix = (jnp.dot(og_ref[...], wo_ref[:GLA_DV, :], preferred_element_type=F32)
           + jnp.dot(os_ref[...], wo_ref[GLA_DV:, :], preferred_element_type=F32))
    _ffn_tail(h_ref[...] + mix, g_ref, wg_ref, wu_ref, wd_ref, out_ref, acc_ref)


def _ffn_specs():
    return [
        _const_spec((1, D_MODEL)),
        _const_spec((D_MODEL, D_FF)),
        _const_spec((D_MODEL, D_FF)),
        _const_spec((D_FF, D_MODEL)),
    ]


def _ffn(h2d, g, wg, wu, wd, tm=512):
    m = h2d.shape[0]
    row = pl.BlockSpec((tm, D_MODEL), lambda i: (i, 0))
    return pl.pallas_call(
        _ffn_kernel,
        grid=(m // tm,),
        in_specs=[row] + _ffn_specs(),
        out_specs=row,
        out_shape=jax.ShapeDtypeStruct((m, D_MODEL), F32),
        scratch_shapes=[pltpu.VMEM((tm, D_MODEL), F32)],
        compiler_params=_cparams(("parallel",), 56),
        name="ffn",
    )(h2d, g, wg, wu, wd)


def _outproj_ffn(h2d, o_gla, o_sb, wo, g, wg, wu, wd, tm=512):
    m = h2d.shape[0]
    row = pl.BlockSpec((tm, D_MODEL), lambda i: (i, 0))
    half = pl.BlockSpec((tm, GLA_DV), lambda i: (i, 0))
    return pl.pallas_call(
        _outproj_ffn_kernel,
        grid=(m // tm,),
        in_specs=[row, half, half, _const_spec((D_MODEL, D_MODEL))] + _ffn_specs(),
        out_specs=row,
        out_shape=jax.ShapeDtypeStruct((m, D_MODEL), F32),
        scratch_shapes=[pltpu.VMEM((tm, D_MODEL), F32)],
        compiler_params=_cparams(("parallel",), 56),
        name="outproj_ffn",
    )(h2d, o_gla, o_sb, wo, g, wg, wu, wd)


HALO = 32
CONV_ROWS = 128
CONV_COLS = 256


def _conformer_kernel(h_ref, g_ref, w1_ref, b1_ref, wdw_ref, bdw_ref, lng_ref, lnb_ref,
                      w2_ref, b2_ref, out_ref, u_ref, y_ref):
    tc = h_ref.shape[0]

    @pl.when(pl.program_id(1) == 0)
    def _():
        u_ref[:HALO, :] = jnp.zeros((HALO, D_MODEL), F32)

    x = h_ref[...]
    xb = _rms(x, g_ref[...]).astype(BF16)
    for c in range(D_MODEL // CONV_COLS):
        cols = slice(c * CONV_COLS, (c + 1) * CONV_COLS)
        gcols = slice(D_MODEL + c * CONV_COLS, D_MODEL + (c + 1) * CONV_COLS)
        lin = jnp.dot(xb, w1_ref[:, cols], preferred_element_type=F32) + b1_ref[:, cols]
        gate = jnp.dot(xb, w1_ref[:, gcols], preferred_element_type=F32) + b1_ref[:, gcols]
        u_ref[HALO:, cols] = lin * _sigmoid(gate)

    shift = HALO - (CONV_WIDTH - 1)
    for c in range(D_MODEL // CONV_COLS):
        cols = slice(c * CONV_COLS, (c + 1) * CONV_COLS)
        for r in range(tc // CONV_ROWS):
            r0 = r * CONV_ROWS
            acc = jnp.broadcast_to(bdw_ref[:, cols], (CONV_ROWS, CONV_COLS))
            for j in range(CONV_WIDTH):
                acc = acc + wdw_ref[j:j + 1, cols] * u_ref[r0 + shift + j:r0 + shift + j + CONV_ROWS, cols]
            y_ref[r0:r0 + CONV_ROWS, cols] = acc

    u_ref[:HALO, :] = u_ref[tc:tc + HALO, :]

    y = y_ref[...]
    mu = jnp.mean(y, axis=-1, keepdims=True)
    d = y - mu
    var = jnp.mean(d * d, axis=-1, keepdims=True)
    yn = d * lax.rsqrt(var + EPS) * lng_ref[...] + lnb_ref[...]
    s = (yn * _sigmoid(yn)).astype(BF16)
    out_ref[...] = x + jnp.dot(s, w2_ref[...], preferred_element_type=F32) + b2_ref[...]


def _conformer(h2d, g, w1, b1, wdw, bdw, lng, lnb, w2, b2, batch, seq, tc=512):
    nt = seq // tc
    m = batch * seq
    row = pl.BlockSpec((tc, D_MODEL), lambda b, i: (b * nt + i, 0))
    return pl.pallas_call(
        _conformer_kernel,
        grid=(batch, nt),
        in_specs=[
            row,
            _const_spec((1, D_MODEL)),
            _const_spec((D_MODEL, 2 * D_MODEL)),
            _const_spec((1, 2 * D_MODEL)),
            _const_spec((HALO, D_MODEL)),
            _const_spec((1, D_MODEL)),
            _const_spec((1, D_MODEL)),
            _const_spec((1, D_MODEL)),
            _const_spec((D_MODEL, D_MODEL)),
            _const_spec((1, D_MODEL)),
        ],
        out_specs=row,
        out_shape=jax.ShapeDtypeStruct((m, D_MODEL), F32),
        scratch_shapes=[pltpu.VMEM((HALO + tc, D_MODEL), F32), pltpu.VMEM((tc, D_MODEL), F32)],
        compiler_params=_cparams(("arbitrary", "arbitrary"), 48),
        name="conformer",
    )(h2d, g, w1, b1, wdw, bdw, lng, lnb, w2, b2)


def kernel(x, mix_norm, ffn_norm, hy_w_in, hy_w_gate2, hy_b_gate, hy_gla_norm, hy_sb_q_norm,
           hy_sb_k_norm, hy_w_out, cv_w_pw1, cv_b_pw1, cv_w_dw, cv_b_dw, cv_ln_g, cv_ln_b,
           cv_w_pw2, cv_b_pw2, ffn_w_gate, ffn_w_up, ffn_w_down):
    batch, seq, _ = x.shape
    m = batch * seq
    h = x.reshape(m, D_MODEL)

    w_in = hy_w_in[0]
    cuts = np.cumsum([0, GLA_DK, GLA_DK, GLA_DV, GLA_DV, GLA_GATE_RANK, SB_D, SB_D, SB_D])
    w_gq, w_gk, w_gv, w_gr, w_glr, w_sq, w_sk, w_sv = (
        w_in[:, cuts[i]:cuts[i + 1]] for i in range(8))
    w_row = jnp.concatenate([w_gv, w_gr, w_sq, w_sk, w_sv, w_gq], axis=1).astype(BF16)
    w_kt = w_gk.T.astype(BF16)
    w_gt = jnp.pad(w_glr.T, ((0, LANES - GLA_GATE_RANK), (0, 0))).astype(BF16)
    proj, kt, glrt = _inproj(h, mix_norm[0][None, :], w_row, w_kt, w_gt)

    w2t = jnp.pad(hy_w_gate2[0].T, ((0, 0), (0, LANES - GLA_GATE_RANK))).astype(BF16)
    bg = jnp.broadcast_to(hy_b_gate[0][:, None], (GLA_DK, LANES)).astype(F32)
    cm = jnp.asarray(_gla_cumsum_matrix(), dtype=BF16)
    o_gla = _gla(proj, kt, glrt, w2t, bg, cm, hy_gla_norm[0][None, :], batch, seq)

    gq2 = jnp.tile(hy_sb_q_norm[0], 2)[None, :]
    gk2 = jnp.tile(hy_sb_k_norm[0], 2)[None, :]
    qn, kn, vb = _sb_prep(proj, gq2, gk2)
    o_sb = _sb_attention(qn, kn, vb, batch, seq)

    h = _outproj_ffn(h, o_gla, o_sb, hy_w_out[0].astype(BF16), ffn_norm[0][None, :],
                     ffn_w_gate[0].astype(BF16), ffn_w_up[0].astype(BF16),
                     ffn_w_down[0].astype(BF16))

    wdw = jnp.pad(cv_w_dw[0], ((0, HALO - CONV_WIDTH), (0, 0)))
    h = _conformer(h, mix_norm[1][None, :], cv_w_pw1[0].astype(BF16), cv_b_pw1[0][None, :],
                   wdw, cv_b_dw[0][None, :], cv_ln_g[0][None, :], cv_ln_b[0][None, :],
                   cv_w_pw2[0].astype(BF16), cv_b_pw2[0][None, :], batch, seq)
    h = _ffn(h, ffn_norm[1][None, :], ffn_w_gate[1].astype(BF16), ffn_w_up[1].astype(BF16),
             ffn_w_down[1].astype(BF16))
    return h.reshape(batch, seq, D_MODEL)
```

```python
import functools

import numpy as np
import jax
import jax.numpy as jnp
from jax import lax
from jax.experimental import pallas as pl
from jax.experimental.pallas import tpu as pltpu

F32 = jnp.float32
BF16 = jnp.bfloat16
ACT_DTYPE = BF16

D_MODEL = 1024
CHUNK = 64
EPS = 1e-6
GLA_HEADS = 4
GLA_HEAD_K = 64
GLA_HEAD_V = 128
GLA_DK = GLA_HEADS * GLA_HEAD_K
GLA_DV = GLA_HEADS * GLA_HEAD_V
GLA_GATE_RANK = 16
GLA_GATE_NORMALIZER = 16.0
SB_HEADS = 8
SB_HEAD_DIM = 64
SB_D = SB_HEADS * SB_HEAD_DIM
CONV_WIDTH = 31
D_FF = 2816

LOG2_E = 1.4426950408889634
LANES = 128
SUBLANES = 8
MIB = 1024 * 1024
PROJ_W = GLA_DV + GLA_DV + GLA_DK
COL_GV, COL_GR, COL_GQ = 0, 512, 1024

NT_DIMS = (((1,), (1,)), ((), ()))


def _cparams(sem, vmem_mib):
    return pltpu.CompilerParams(dimension_semantics=sem, vmem_limit_bytes=vmem_mib * MIB)


def _rms(x, g):
    return x * lax.rsqrt(jnp.mean(x * x, axis=-1, keepdims=True) + EPS) * g


def _sigmoid(x):
    return 1.0 / (1.0 + jnp.exp(-x))


def _log_sigmoid(x):
    return jnp.minimum(x, 0.0) - jnp.log(1.0 + jnp.exp(-jnp.abs(x)))


def _split_bf16(x):
    hi = x.astype(BF16)
    lo = (x - hi.astype(F32)).astype(BF16)
    return hi, lo


def _const_spec(shape):
    nd = len(shape)
    return pl.BlockSpec(shape, lambda *_: (0,) * nd, pipeline_mode=pl.Buffered(1))


def _head_pair_norm(x, g, first):
    x2 = x * x
    s0 = jnp.sum(jnp.where(first, x2, 0.0), axis=-1, keepdims=True)
    s1 = jnp.sum(jnp.where(first, 0.0, x2), axis=-1, keepdims=True)
    ms = jnp.where(first, s0, s1) * (1.0 / SB_HEAD_DIM)
    return x * lax.rsqrt(ms + EPS) * g


def _inproj_kernel(x_ref, g_ref, w_ref, wsb_ref, wkt_ref, wgt_ref, gq_ref, gk_ref,
                   proj_ref, qn_ref, kn_ref, vb_ref, kt_ref, glrt_ref):
    tm = x_ref.shape[0]
    xb = _rms(x_ref[...], g_ref[...]).astype(BF16)
    proj_ref[...] = jnp.dot(xb, w_ref[...], preferred_element_type=F32)
    kt_ref[...] = lax.dot_general(wkt_ref[...], xb, NT_DIMS, preferred_element_type=F32)
    glrt_ref[...] = lax.dot_general(wgt_ref[...], xb, NT_DIMS, preferred_element_type=F32)
    first = lax.broadcasted_iota(jnp.int32, (tm, LANES), 1) < SB_HEAD_DIM
    q_scale = SB_HEAD_DIM ** -0.5 * LOG2_E
    wide = 2 * LANES
    for c in range(SB_D // wide):
        sq = jnp.dot(xb, wsb_ref[:, c * wide:(c + 1) * wide], preferred_element_type=F32)
        sk = jnp.dot(xb, wsb_ref[:, SB_D + c * wide:SB_D + (c + 1) * wide],
                     preferred_element_type=F32)
        for p in range(2):
            src = slice(p * LANES, (p + 1) * LANES)
            dst = slice(c * wide + p * LANES, c * wide + (p + 1) * LANES)
            qn = _head_pair_norm(sq[:, src], gq_ref[...], first) * q_scale
            qn_ref[:, dst] = qn.astype(qn_ref.dtype)
            kn_ref[:, dst] = _head_pair_norm(sk[:, src], gk_ref[...], first).astype(kn_ref.dtype)
    vb_ref[...] = jnp.dot(xb, wsb_ref[:, 2 * SB_D:], preferred_element_type=F32).astype(vb_ref.dtype)


def _inproj(x2d, g, w, wsb, wkt, wgt, gq2, gk2, tm=512):
    m = x2d.shape[0]
    sb_spec = pl.BlockSpec((tm, SB_D), lambda i: (i, 0))
    sb_shape = jax.ShapeDtypeStruct((m, SB_D), ACT_DTYPE)
    return pl.pallas_call(
        _inproj_kernel,
        grid=(m // tm,),
        in_specs=[
            pl.BlockSpec((tm, D_MODEL), lambda i: (i, 0)),
            _const_spec((1, D_MODEL)),
            _const_spec((D_MODEL, PROJ_W)),
            _const_spec((D_MODEL, 3 * SB_D)),
            _const_spec((GLA_DK, D_MODEL)),
            _const_spec((LANES, D_MODEL)),
            _const_spec((1, LANES)),
            _const_spec((1, LANES)),
        ],
        out_specs=[
            pl.BlockSpec((tm, PROJ_W), lambda i: (i, 0)),
            sb_spec, sb_spec, sb_spec,
            pl.BlockSpec((GLA_DK, tm), lambda i: (0, i)),
            pl.BlockSpec((LANES, tm), lambda i: (0, i)),
        ],
        out_shape=[
            jax.ShapeDtypeStruct((m, PROJ_W), F32),
            sb_shape, sb_shape, sb_shape,
            jax.ShapeDtypeStruct((GLA_DK, m), F32),
            jax.ShapeDtypeStruct((LANES, m), F32),
        ],
        compiler_params=_cparams(("parallel",), 48),
        name="inproj",
    )(x2d, g, w, wsb, wkt, wgt, gq2, gk2)


GLA_TILE = 2 * CHUNK


def _gla_cumsum_matrix():
    r = np.arange(GLA_TILE)[:, None]
    i = np.arange(GLA_TILE)[None, :]
    same = (r // CHUNK) == (i // CHUNK)
    rem = (r > i) & same
    tot0 = np.broadcast_to(r < CHUNK, (GLA_TILE, GLA_TILE))
    tot1 = np.broadcast_to(r >= CHUNK, (GLA_TILE, GLA_TILE))
    return np.concatenate([rem, tot0, tot1], axis=1).astype(np.float32)


def _gla_kernel(*refs, tiles, batch):
    seq_refs = [refs[5 * b:5 * b + 5] for b in range(batch)]
    w2t_ref, bg_ref, cm_ref, gout_ref, o_ref, s_ref = refs[5 * batch:]

    @pl.when(pl.program_id(0) == 0)
    def _():
        s_ref[...] = jnp.zeros_like(s_ref)

    lane = lax.broadcasted_iota(jnp.int32, (GLA_DK, GLA_TILE), 1)
    for t in range(tiles):
        cols = slice(t * GLA_TILE, (t + 1) * GLA_TILE)
        k_half, decay, v = [], [], []
        for b in range(batch):
            _, v_ref, _, kt_ref, glrt_ref = seq_refs[b]
            glr = glrt_ref[:, cols].astype(BF16)
            pre = jnp.dot(w2t_ref[...], glr, preferred_element_type=F32) + bg_ref[...]
            la = _log_sigmoid(pre) * (1.0 / GLA_GATE_NORMALIZER)
            hi, lo = _split_bf16(la)
            sums = (jnp.dot(hi, cm_ref[...], preferred_element_type=F32)
                    + jnp.dot(lo, cm_ref[...], preferred_element_type=F32))
            k_end = (kt_ref[:, cols] * jnp.exp(sums[:, :GLA_TILE])).astype(BF16)
            decay.append((jnp.exp(sums[:, GLA_TILE:2 * GLA_TILE]),
                          jnp.exp(sums[:, 2 * GLA_TILE:])))
            k_half.append((jnp.where(lane < CHUNK, k_end, jnp.zeros_like(k_end)),
                           jnp.where(lane >= CHUNK, k_end, jnp.zeros_like(k_end))))
            v.append(v_ref[cols, :].astype(BF16))
        for j in range(2):
            rows = slice(t * GLA_TILE + j * CHUNK, t * GLA_TILE + (j + 1) * CHUNK)
            for b in range(batch):
                q_ref, _, r_ref, _, _ = seq_refs[b]
                for h in range(GLA_HEADS):
                    kr = slice(h * GLA_HEAD_K, (h + 1) * GLA_HEAD_K)
                    vc = slice(h * GLA_HEAD_V, (h + 1) * GLA_HEAD_V)
                    upd = jnp.dot(k_half[b][j][kr, :], v[b][:, vc], preferred_element_type=F32)
                    s_ref[b, kr, vc] = decay[b][j][kr, :] * s_ref[b, kr, vc] + upd
                q = q_ref[rows, :].astype(BF16)
                o = jnp.dot(q, s_ref[b].astype(BF16), preferred_element_type=F32)
                o = o * (GLA_HEAD_K ** -0.5)
                gate = r_ref[rows, :]
                for h in range(GLA_HEADS):
                    vc = slice(h * GLA_HEAD_V, (h + 1) * GLA_HEAD_V)
                    oh = _rms(o[:, vc], gout_ref[...])
                    g = gate[:, vc]
                    o_ref[b, rows, vc] = (oh * (g * _sigmoid(g))).astype(o_ref.dtype)


def _gla(proj, kt, glrt, w2t, bg, cm, gout, batch, seq, tr=256):
    nt = seq // tr
    kernel = functools.partial(_gla_kernel, tiles=tr // GLA_TILE, batch=batch)
    seq_specs, seq_args = [], []
    for b in range(batch):
        row = lambda i, b=b: b * nt + i
        seq_specs += [
            pl.BlockSpec((tr, GLA_DK), lambda i, row=row: (row(i), COL_GQ // GLA_DK)),
            pl.BlockSpec((tr, GLA_DV), lambda i, row=row: (row(i), COL_GV // GLA_DV)),
            pl.BlockSpec((tr, GLA_DV), lambda i, row=row: (row(i), COL_GR // GLA_DV)),
            pl.BlockSpec((GLA_DK, tr), lambda i, row=row: (0, row(i))),
            pl.BlockSpec((LANES, tr), lambda i, row=row: (0, row(i))),
        ]
        seq_args += [proj, proj, proj, kt, glrt]
    out = pl.pallas_call(
        kernel,
        grid=(nt,),
        in_specs=seq_specs + [
            _const_spec((GLA_DK, LANES)),
            _const_spec((GLA_DK, LANES)),
            _const_spec((GLA_TILE, 3 * GLA_TILE)),
            _const_spec((1, GLA_HEAD_V)),
        ],
        out_specs=pl.BlockSpec((batch, tr, GLA_DV), lambda i: (0, i, 0)),
        out_shape=jax.ShapeDtypeStruct((batch, seq, GLA_DV), ACT_DTYPE),
        scratch_shapes=[pltpu.VMEM((batch, GLA_DK, GLA_DV), F32)],
        compiler_params=_cparams(("arbitrary",), 32),
        name="gla",
    )(*seq_args, w2t, bg, cm, gout)
    return out.reshape(batch * seq, GLA_DV)


SB_TQ = 256
SB_DEAD_LOG2 = 160.0
SB_NO_BLOCK = 1e30


def _sb_kernel(q_ref, k_ref, v_ref, o_ref, acc_ref, c_ref):
    qi = pl.program_id(2)
    tq = SB_TQ
    lane = lax.broadcasted_iota(jnp.int32, (tq, LANES), 1)
    q = q_ref[...].astype(BF16)
    zero = jnp.zeros_like(q)
    q_heads = (jnp.where(lane < SB_HEAD_DIM, q, zero), jnp.where(lane >= SB_HEAD_DIM, q, zero))
    row = lax.broadcasted_iota(jnp.int32, (tq, tq), 0)
    col = lax.broadcasted_iota(jnp.int32, (tq, tq), 1)
    later = jnp.where(row > col, 1.0, 0.0).astype(BF16)
    past = col < row

    def load(kb):
        start = pl.multiple_of(kb * tq, tq)
        return k_ref[pl.ds(start, tq), :].astype(BF16), v_ref[pl.ds(start, tq), :].astype(BF16)

    def neg_log2_keep(h, kblk):
        y = lax.dot_general(q_heads[h], kblk, NT_DIMS, preferred_element_type=F32)
        return y, jnp.maximum(y, 0.0) + jnp.log2(1.0 + jnp.exp2(-jnp.abs(y)))

    def weights(y, p, carry):
        hi, lo = _split_bf16(p)
        total = p + (jnp.dot(hi, later, preferred_element_type=F32)
                     + jnp.dot(lo, later, preferred_element_type=F32))
        if carry is not None:
            total = total + carry
        return jnp.exp2(y - total)

    def pv(w, vblk):
        return jnp.dot(w.astype(BF16), vblk, preferred_element_type=F32)

    kd, vd = load(qi)
    kp, vp = load(jnp.maximum(qi - 1, 0))
    no_prev = jnp.where(qi > 0, 0.0, SB_NO_BLOCK)
    carry_min = None
    for h in range(2):
        yd, pd = neg_log2_keep(h, kd)
        pd = jnp.where(past, pd, 0.0)
        wd = jnp.where(past, weights(yd, pd, None), 0.0)
        cd = jnp.sum(pd, axis=-1, keepdims=True)
        yp, pp = neg_log2_keep(h, kp)
        wp = weights(yp, pp, cd + no_prev)
        acc_ref[h] = pv(wd, vd) + pv(wp, vp)
        c = cd + jnp.sum(pp, axis=-1, keepdims=True)
        c_ref[h] = jnp.broadcast_to(c, (tq, LANES))
        m = jnp.min(c)
        carry_min = m if carry_min is None else jnp.minimum(carry_min, m)

    def cond(state):
        kb, alive = state
        return jnp.logical_and(kb >= 0, alive > 0)

    def body(state):
        kb, _ = state
        kblk, vblk = load(kb)
        cmin = None
        for h in range(2):
            y, p = neg_log2_keep(h, kblk)
            c = c_ref[h]
            w = weights(y, p, jnp.concatenate([c, c], axis=1))
            acc_ref[h] += pv(w, vblk)
            c = c + jnp.sum(p, axis=-1, keepdims=True)
            c_ref[h] = c
            m = jnp.min(c)
            cmin = m if cmin is None else jnp.minimum(cmin, m)
        return kb - 1, (cmin < SB_DEAD_LOG2).astype(jnp.int32)

    lax.while_loop(cond, body, (qi - 2, (carry_min < SB_DEAD_LOG2).astype(jnp.int32)))
    o_ref[...] = jnp.where(lane < SB_HEAD_DIM, acc_ref[0], acc_ref[1]).astype(o_ref.dtype)


def _sb_attention(qn, kn, vb, batch, seq):
    nq = seq // SB_TQ
    pairs = SB_D // LANES
    q3, k3, v3 = (a.reshape(batch, seq, SB_D) for a in (qn, kn, vb))
    out = pl.pallas_call(
        _sb_kernel,
        grid=(batch, pairs, nq),
        in_specs=[
            pl.BlockSpec((None, SB_TQ, LANES), lambda b, p, i: (b, i, p)),
            pl.BlockSpec((None, seq, LANES), lambda b, p, i: (b, 0, p)),
            pl.BlockSpec((None, seq, LANES), lambda b, p, i: (b, 0, p)),
        ],
        out_specs=pl.BlockSpec((None, SB_TQ, LANES), lambda b, p, i: (b, i, p)),
        out_shape=jax.ShapeDtypeStruct((batch, seq, SB_D), ACT_DTYPE),
        scratch_shapes=[pltpu.VMEM((2, SB_TQ, LANES), F32), pltpu.VMEM((2, SB_TQ, LANES), F32)],
        compiler_params=_cparams(("parallel", "parallel", "arbitrary"), 32),
        name="sb_attention",
    )(q3, k3, v3)
    return out.reshape(batch * seq, SB_D)


FF_CHUNK = 256


def _ffn_tail(h1, g_ref, wg_ref, wu_ref, wd_ref, out_ref, acc_ref):
    hb = _rms(h1, g_ref[...]).astype(BF16)
    acc_ref[...] = h1
    for c in range(D_FF // FF_CHUNK):
        cols = slice(c * FF_CHUNK, (c + 1) * FF_CHUNK)
        a = jnp.dot(hb, wg_ref[:, cols], preferred_element_type=F32)
        u = jnp.dot(hb, wu_ref[:, cols], preferred_element_type=F32)
        act = (a * _sigmoid(a) * u).astype(BF16)
        acc_ref[...] += jnp.dot(act, wd_ref[cols, :], preferred_element_type=F32)
    out_ref[...] = acc_ref[...]


def _outproj_ffn_kernel(h_ref, og_ref, os_ref, wo_ref, g_ref, wg_ref, wu_ref, wd_ref,
                        out_ref, acc_ref):
    mix = (jnp.dot(og_ref[...].astype(BF16), wo_ref[:GLA_DV, :], preferred_element_type=F32)
           + jnp.dot(os_ref[...].astype(BF16), wo_ref[GLA_DV:, :], preferred_element_type=F32))
    _ffn_tail(h_ref[...] + mix, g_ref, wg_ref, wu_ref, wd_ref, out_ref, acc_ref)


def _ffn_specs():
    return [
        _const_spec((1, D_MODEL)),
        _const_spec((D_MODEL, D_FF)),
        _const_spec((D_MODEL, D_FF)),
        _const_spec((D_FF, D_MODEL)),
    ]


def _outproj_ffn(h2d, o_gla, o_sb, wo, g, wg, wu, wd, tm=512):
    m = h2d.shape[0]
    row = pl.BlockSpec((tm, D_MODEL), lambda i: (i, 0))
    half = pl.BlockSpec((tm, GLA_DV), lambda i: (i, 0))
    return pl.pallas_call(
        _outproj_ffn_kernel,
        grid=(m // tm,),
        in_specs=[row, half, half, _const_spec((D_MODEL, D_MODEL))] + _ffn_specs(),
        out_specs=row,
        out_shape=jax.ShapeDtypeStruct((m, D_MODEL), F32),
        scratch_shapes=[pltpu.VMEM((tm, D_MODEL), F32)],
        compiler_params=_cparams(("parallel",), 56),
        name="outproj_ffn",
    )(h2d, o_gla, o_sb, wo, g, wg, wu, wd)


HALO = 32
CONV_ROWS = 128
CONV_COLS = 256


def _interleave(first, second):
    lists = (first, second)
    total = [sum(cost for cost, _ in lst) for lst in lists]
    done, idx = [0.0, 0.0], [0, 0]
    while idx[0] < len(first) or idx[1] < len(second):
        if idx[1] >= len(second):
            k = 0
        elif idx[0] >= len(first):
            k = 1
        else:
            k = 0 if done[0] / total[0] <= done[1] / total[1] else 1
        cost, thunk = lists[k][idx[k]]
        thunk()
        done[k] += cost
        idx[k] += 1


def _layer1_kernel(h_ref, g_ref, w1_ref, b1_ref, wdw_ref, bdw_ref, lng_ref, lnb_ref,
                   w2_ref, b2_ref, gf_ref, wg_ref, wu_ref, wd_ref, out_ref,
                   u_ref, y_ref, sh_ref, mid_ref, acc_ref, *, tiles_per_seq):
    s = pl.program_id(0)
    tc = h_ref.shape[0]

    @pl.when(s == 0)
    def _():
        mid_ref[...] = jnp.zeros_like(mid_ref)

    @pl.when(s % tiles_per_seq == 0)
    def _():
        u_ref[:HALO, :] = jnp.zeros((HALO, D_MODEL), F32)

    prev = mid_ref[(s + 1) % 2]
    hb = _rms(prev, gf_ref[...]).astype(BF16)
    acc_ref[...] = prev

    def ffn_chunk(c):
        cols = slice(c * FF_CHUNK, (c + 1) * FF_CHUNK)
        a = jnp.dot(hb, wg_ref[:, cols], preferred_element_type=F32)
        u = jnp.dot(hb, wu_ref[:, cols], preferred_element_type=F32)
        act = (a * _sigmoid(a) * u).astype(BF16)
        acc_ref[...] += jnp.dot(act, wd_ref[cols, :], preferred_element_type=F32)

    ffn_pieces = [(1270, functools.partial(ffn_chunk, c)) for c in range(D_FF // FF_CHUNK)]

    xb = _rms(h_ref[...], g_ref[...]).astype(BF16)
    shift = HALO - (CONV_WIDTH - 1)
    span = HALO + tc

    def glu(c):
        cols = slice(c * CONV_COLS, (c + 1) * CONV_COLS)
        gcols = slice(D_MODEL + c * CONV_COLS, D_MODEL + (c + 1) * CONV_COLS)
        lin = jnp.dot(xb, w1_ref[:, cols], preferred_element_type=F32) + b1_ref[:, cols]
        gate = jnp.dot(xb, w1_ref[:, gcols], preferred_element_type=F32) + b1_ref[:, gcols]
        u_ref[HALO:, cols] = lin * _sigmoid(gate)

    def shifted_copies(c):
        slab = u_ref[:, c * CONV_COLS:(c + 1) * CONV_COLS]
        for r in range(1, SUBLANES):
            sh_ref[r - 1] = pltpu.roll(slab, span - r, axis=0)

    def conv(c, rb):
        cols = slice(c * CONV_COLS, (c + 1) * CONV_COLS)
        r0 = rb * CONV_ROWS
        acc = jnp.broadcast_to(bdw_ref[:, cols], (CONV_ROWS, CONV_COLS))
        for j in range(CONV_WIDTH):
            a, r = divmod(shift + j, SUBLANES)
            rows = slice(r0 + SUBLANES * a, r0 + SUBLANES * a + CONV_ROWS)
            tap = u_ref[rows, cols] if r == 0 else sh_ref[r - 1, rows, :]
            acc = acc + wdw_ref[j:j + 1, cols] * tap
        y_ref[r0:r0 + CONV_ROWS, cols] = acc

    def carry_halo():
        u_ref[:HALO, :] = u_ref[tc:tc + HALO, :]

    def norm_project(rb):
        rows = slice(rb * CONV_ROWS, (rb + 1) * CONV_ROWS)
        y = y_ref[rows, :]
        mu = jnp.mean(y, axis=-1, keepdims=True)
        d = y - mu
        var = jnp.mean(d * d, axis=-1, keepdims=True)
        yn = d * lax.rsqrt(var + EPS) * lng_ref[...] + lnb_ref[...]
        act = (yn * _sigmoid(yn)).astype(BF16)
        mid_ref[s % 2, rows, :] = (h_ref[rows, :] + b2_ref[...]
                                   + jnp.dot(act, w2_ref[...], preferred_element_type=F32))

    conf_pieces = []
    for c in range(D_MODEL // CONV_COLS):
        conf_pieces.append((270, functools.partial(glu, c)))
        conf_pieces.append((800, functools.partial(shifted_copies, c)))
        conf_pieces += [(520, functools.partial(conv, c, rb)) for rb in range(tc // CONV_ROWS)]
    conf_pieces.append((10, carry_halo))
    conf_pieces += [(750, functools.partial(norm_project, rb)) for rb in range(tc // CONV_ROWS)]

    _interleave(conf_pieces, ffn_pieces)
    out_ref[...] = acc_ref[...]


def _layer1(h2d, g, w1, b1, wdw, bdw, lng, lnb, w2, b2, gf, wg, wu, wd, seq, tc=512):
    m = h2d.shape[0]
    tiles = m // tc
    kernel = functools.partial(_layer1_kernel, tiles_per_seq=seq // tc)
    return pl.pallas_call(
        kernel,
        grid=(tiles + 1,),
        in_specs=[
            pl.BlockSpec((tc, D_MODEL), lambda s: (jnp.minimum(s, tiles - 1), 0)),
            _const_spec((1, D_MODEL)),
            _const_spec((D_MODEL, 2 * D_MODEL)),
            _const_spec((1, 2 * D_MODEL)),
            _const_spec((HALO, D_MODEL)),
            _const_spec((1, D_MODEL)),
            _const_spec((1, D_MODEL)),
            _const_spec((1, D_MODEL)),
            _const_spec((D_MODEL, D_MODEL)),
            _const_spec((1, D_MODEL)),
        ] + _ffn_specs(),
        out_specs=pl.BlockSpec((tc, D_MODEL), lambda s: (jnp.maximum(s - 1, 0), 0)),
        out_shape=jax.ShapeDtypeStruct((m, D_MODEL), F32),
        scratch_shapes=[
            pltpu.VMEM((HALO + tc, D_MODEL), F32),
            pltpu.VMEM((tc, D_MODEL), F32),
            pltpu.VMEM((SUBLANES - 1, HALO + tc, CONV_COLS), F32),
            pltpu.VMEM((2, tc, D_MODEL), F32),
            pltpu.VMEM((tc, D_MODEL), F32),
        ],
        compiler_params=_cparams(("arbitrary",), 58),
        name="layer1",
    )(h2d, g, w1, b1, wdw, bdw, lng, lnb, w2, b2, gf, wg, wu, wd)


def kernel(x, mix_norm, ffn_norm, hy_w_in, hy_w_gate2, hy_b_gate, hy_gla_norm, hy_sb_q_norm,
           hy_sb_k_norm, hy_w_out, cv_w_pw1, cv_b_pw1, cv_w_dw, cv_b_dw, cv_ln_g, cv_ln_b,
           cv_w_pw2, cv_b_pw2, ffn_w_gate, ffn_w_up, ffn_w_down):
    batch, seq, _ = x.shape
    m = batch * seq
    h = x.reshape(m, D_MODEL)

    w_in = hy_w_in[0]
    cuts = np.cumsum([0, GLA_DK, GLA_DK, GLA_DV, GLA_DV, GLA_GATE_RANK, SB_D, SB_D, SB_D])
    w_gq, w_gk, w_gv, w_gr, w_glr, w_sq, w_sk, w_sv = (
        w_in[:, cuts[i]:cuts[i + 1]] for i in range(8))
    w_row = jnp.concatenate([w_gv, w_gr, w_gq], axis=1).astype(BF16)
    w_sb = w_in[:, cuts[5]:cuts[8]].astype(BF16)
    w_kt = w_gk.T.astype(BF16)
    w_gt = jnp.pad(w_glr.T, ((0, LANES - GLA_GATE_RANK), (0, 0))).astype(BF16)
    gq2 = jnp.tile(hy_sb_q_norm[0], 2)[None, :]
    gk2 = jnp.tile(hy_sb_k_norm[0], 2)[None, :]
    proj, qn, kn, vb, kt, glrt = _inproj(h, mix_norm[0][None, :], w_row, w_sb, w_kt, w_gt, gq2, gk2)

    w2t = jnp.pad(hy_w_gate2[0].T, ((0, 0), (0, LANES - GLA_GATE_RANK))).astype(BF16)
    bg = jnp.broadcast_to(hy_b_gate[0][:, None], (GLA_DK, LANES)).astype(F32)
    cm = jnp.asarray(_gla_cumsum_matrix(), dtype=BF16)
    o_gla = _gla(proj, kt, glrt, w2t, bg, cm, hy_gla_norm[0][None, :], batch, seq)

    o_sb = _sb_attention(qn, kn, vb, batch, seq)

    h = _outproj_ffn(h, o_gla, o_sb, hy_w_out[0].astype(BF16), ffn_norm[0][None, :],
                     ffn_w_gate[0].astype(BF16), ffn_w_up[0].astype(BF16),
                     ffn_w_down[0].astype(BF16))

    wdw = jnp.pad(cv_w_dw[0], ((0, HALO - CONV_WIDTH), (0, 0)))
    h = _layer1(h, mix_norm[1][None, :], cv_w_pw1[0].astype(BF16), cv_b_pw1[0][None, :],
                wdw, cv_b_dw[0][None, :], cv_ln_g[0][None, :], cv_ln_b[0][None, :],
                cv_w_pw2[0].astype(BF16), cv_b_pw2[0][None, :], ffn_norm[1][None, :],
                ffn_w_gate[1].astype(BF16), ffn_w_up[1].astype(BF16), ffn_w_down[1].astype(BF16), seq)
    return h.reshape(batch, seq, D_MODEL)
```

```python
import functools

import numpy as np
import jax
import jax.numpy as jnp
from jax import lax
from jax.experimental import pallas as pl
from jax.experimental.pallas import tpu as pltpu

F32 = jnp.float32
BF16 = jnp.bfloat16
ACT_DTYPE = BF16

D_MODEL = 1024
CHUNK = 64
EPS = 1e-6
GLA_HEADS = 4
GLA_HEAD_K = 64
GLA_HEAD_V = 128
GLA_DK = GLA_HEADS * GLA_HEAD_K
GLA_DV = GLA_HEADS * GLA_HEAD_V
GLA_GATE_RANK = 16
GLA_GATE_NORMALIZER = 16.0
SB_HEADS = 8
SB_HEAD_DIM = 64
SB_D = SB_HEADS * SB_HEAD_DIM
CONV_WIDTH = 31
D_FF = 2816

LOG2_E = 1.4426950408889634
LANES = 128
SUBLANES = 8
MIB = 1024 * 1024
PROJ_W = GLA_DV + GLA_DV + GLA_DK
COL_GV, COL_GR, COL_GQ = 0, 512, 1024

NT_DIMS = (((1,), (1,)), ((), ()))


def _cparams(sem, vmem_mib):
    return pltpu.CompilerParams(dimension_semantics=sem, vmem_limit_bytes=vmem_mib * MIB)


def _rms(x, g):
    return x * lax.rsqrt(jnp.mean(x * x, axis=-1, keepdims=True) + EPS) * g


def _sigmoid(x):
    return 1.0 / (1.0 + jnp.exp(-x))


def _log_sigmoid(x):
    return jnp.minimum(x, 0.0) - jnp.log(1.0 + jnp.exp(-jnp.abs(x)))


def _split_bf16(x):
    hi = x.astype(BF16)
    lo = (x - hi.astype(F32)).astype(BF16)
    return hi, lo


def _const_spec(shape):
    nd = len(shape)
    return pl.BlockSpec(shape, lambda *_: (0,) * nd, pipeline_mode=pl.Buffered(1))


def _head_pair_norm(x, g, first):
    x2 = x * x
    s0 = jnp.sum(jnp.where(first, x2, 0.0), axis=-1, keepdims=True)
    s1 = jnp.sum(jnp.where(first, 0.0, x2), axis=-1, keepdims=True)
    ms = jnp.where(first, s0, s1) * (1.0 / SB_HEAD_DIM)
    return x * lax.rsqrt(ms + EPS) * g


def _inproj_kernel(x_ref, g_ref, w_ref, wsb_ref, wkt_ref, wgt_ref, gq_ref, gk_ref,
                   proj_ref, qn_ref, kn_ref, vb_ref, kt_ref, glrt_ref):
    tm = x_ref.shape[0]
    xb = _rms(x_ref[...], g_ref[...]).astype(BF16)
    proj_ref[...] = jnp.dot(xb, w_ref[...], preferred_element_type=F32)
    kt_ref[...] = lax.dot_general(wkt_ref[...], xb, NT_DIMS, preferred_element_type=F32)
    glrt_ref[...] = lax.dot_general(wgt_ref[...], xb, NT_DIMS, preferred_element_type=F32)
    first = lax.broadcasted_iota(jnp.int32, (tm, LANES), 1) < SB_HEAD_DIM
    q_scale = SB_HEAD_DIM ** -0.5 * LOG2_E
    wide = 2 * LANES
    for c in range(SB_D // wide):
        sq = jnp.dot(xb, wsb_ref[:, c * wide:(c + 1) * wide], preferred_element_type=F32)
        sk = jnp.dot(xb, wsb_ref[:, SB_D + c * wide:SB_D + (c + 1) * wide],
                     preferred_element_type=F32)
        for p in range(2):
            src = slice(p * LANES, (p + 1) * LANES)
            dst = slice(c * wide + p * LANES, c * wide + (p + 1) * LANES)
            qn = _head_pair_norm(sq[:, src], gq_ref[...], first) * q_scale
            qn_ref[:, dst] = qn.astype(qn_ref.dtype)
            kn_ref[:, dst] = _head_pair_norm(sk[:, src], gk_ref[...], first).astype(kn_ref.dtype)
    vb_ref[...] = jnp.dot(xb, wsb_ref[:, 2 * SB_D:], preferred_element_type=F32).astype(vb_ref.dtype)


def _inproj(x2d, g, w, wsb, wkt, wgt, gq2, gk2, tm=512):
    m = x2d.shape[0]
    sb_spec = pl.BlockSpec((tm, SB_D), lambda i: (i, 0))
    sb_shape = jax.ShapeDtypeStruct((m, SB_D), ACT_DTYPE)
    return pl.pallas_call(
        _inproj_kernel,
        grid=(m // tm,),
        in_specs=[
            pl.BlockSpec((tm, D_MODEL), lambda i: (i, 0)),
            _const_spec((1, D_MODEL)),
            _const_spec((D_MODEL, PROJ_W)),
            _const_spec((D_MODEL, 3 * SB_D)),
            _const_spec((GLA_DK, D_MODEL)),
            _const_spec((LANES, D_MODEL)),
            _const_spec((1, LANES)),
            _const_spec((1, LANES)),
        ],
        out_specs=[
            pl.BlockSpec((tm, PROJ_W), lambda i: (i, 0)),
            sb_spec, sb_spec, sb_spec,
            pl.BlockSpec((GLA_DK, tm), lambda i: (0, i)),
            pl.BlockSpec((LANES, tm), lambda i: (0, i)),
        ],
        out_shape=[
            jax.ShapeDtypeStruct((m, PROJ_W), F32),
            sb_shape, sb_shape, sb_shape,
            jax.ShapeDtypeStruct((GLA_DK, m), F32),
            jax.ShapeDtypeStruct((LANES, m), F32),
        ],
        compiler_params=_cparams(("parallel",), 48),
        name="inproj",
    )(x2d, g, w, wsb, wkt, wgt, gq2, gk2)


GLA_TILE = 2 * CHUNK


def _gla_cumsum_matrix():
    r = np.arange(GLA_TILE)[:, None]
    i = np.arange(GLA_TILE)[None, :]
    same = (r // CHUNK) == (i // CHUNK)
    rem = (r > i) & same
    tot0 = np.broadcast_to(r < CHUNK, (GLA_TILE, GLA_TILE))
    tot1 = np.broadcast_to(r >= CHUNK, (GLA_TILE, GLA_TILE))
    return np.concatenate([rem, tot0, tot1], axis=1).astype(np.float32)


def _gla_kernel(*refs, tiles, batch):
    seq_refs = [refs[5 * b:5 * b + 5] for b in range(batch)]
    w2t_ref, bg_ref, cm_ref, gout_ref, o_ref, s_ref = refs[5 * batch:]

    @pl.when(pl.program_id(0) == 0)
    def _():
        s_ref[...] = jnp.zeros_like(s_ref)

    lane = lax.broadcasted_iota(jnp.int32, (GLA_DK, GLA_TILE), 1)
    for t in range(tiles):
        cols = slice(t * GLA_TILE, (t + 1) * GLA_TILE)
        k_half, decay, v = [], [], []
        for b in range(batch):
            _, v_ref, _, kt_ref, glrt_ref = seq_refs[b]
            glr = glrt_ref[:, cols].astype(BF16)
            pre = jnp.dot(w2t_ref[...], glr, preferred_element_type=F32) + bg_ref[...]
            la = _log_sigmoid(pre) * (1.0 / GLA_GATE_NORMALIZER)
            hi, lo = _split_bf16(la)
            sums = (jnp.dot(hi, cm_ref[...], preferred_element_type=F32)
                    + jnp.dot(lo, cm_ref[...], preferred_element_type=F32))
            k_end = (kt_ref[:, cols] * jnp.exp(sums[:, :GLA_TILE])).astype(BF16)
            decay.append((jnp.exp(sums[:, GLA_TILE:2 * GLA_TILE]),
                          jnp.exp(sums[:, 2 * GLA_TILE:])))
            k_half.append((jnp.where(lane < CHUNK, k_end, jnp.zeros_like(k_end)),
                           jnp.where(lane >= CHUNK, k_end, jnp.zeros_like(k_end))))
            v.append(v_ref[cols, :].astype(BF16))
        for j in range(2):
            rows = slice(t * GLA_TILE + j * CHUNK, t * GLA_TILE + (j + 1) * CHUNK)
            for b in range(batch):
                q_ref, _, r_ref, _, _ = seq_refs[b]
                for h in range(GLA_HEADS):
                    kr = slice(h * GLA_HEAD_K, (h + 1) * GLA_HEAD_K)
                    vc = slice(h * GLA_HEAD_V, (h + 1) * GLA_HEAD_V)
                    upd = jnp.dot(k_half[b][j][kr, :], v[b][:, vc], preferred_element_type=F32)
                    s_ref[b, kr, vc] = decay[b][j][kr, :] * s_ref[b, kr, vc] + upd
                q = q_ref[rows, :].astype(BF16)
                o = jnp.dot(q, s_ref[b].astype(BF16), preferred_element_type=F32)
                o = o * (GLA_HEAD_K ** -0.5)
                gate = r_ref[rows, :]
                for h in range(GLA_HEADS):
                    vc = slice(h * GLA_HEAD_V, (h + 1) * GLA_HEAD_V)
                    oh = _rms(o[:, vc], gout_ref[...])
                    g = gate[:, vc]
                    o_ref[b, rows, vc] = (oh * (g * _sigmoid(g))).astype(o_ref.dtype)


def _gla(proj, kt, glrt, w2t, bg, cm, gout, batch, seq, tr=256):
    nt = seq // tr
    kernel = functools.partial(_gla_kernel, tiles=tr // GLA_TILE, batch=batch)
    seq_specs, seq_args = [], []
    for b in range(batch):
        row = lambda i, b=b: b * nt + i
        seq_specs += [
            pl.BlockSpec((tr, GLA_DK), lambda i, row=row: (row(i), COL_GQ // GLA_DK)),
            pl.BlockSpec((tr, GLA_DV), lambda i, row=row: (row(i), COL_GV // GLA_DV)),
            pl.BlockSpec((tr, GLA_DV), lambda i, row=row: (row(i), COL_GR // GLA_DV)),
            pl.BlockSpec((GLA_DK, tr), lambda i, row=row: (0, row(i))),
            pl.BlockSpec((LANES, tr), lambda i, row=row: (0, row(i))),
        ]
        seq_args += [proj, proj, proj, kt, glrt]
    out = pl.pallas_call(
        kernel,
        grid=(nt,),
        in_specs=seq_specs + [
            _const_spec((GLA_DK, LANES)),
            _const_spec((GLA_DK, LANES)),
            _const_spec((GLA_TILE, 3 * GLA_TILE)),
            _const_spec((1, GLA_HEAD_V)),
        ],
        out_specs=pl.BlockSpec((batch, tr, GLA_DV), lambda i: (0, i, 0)),
        out_shape=jax.ShapeDtypeStruct((batch, seq, GLA_DV), ACT_DTYPE),
        scratch_shapes=[pltpu.VMEM((batch, GLA_DK, GLA_DV), F32)],
        compiler_params=_cparams(("arbitrary",), 32),
        name="gla",
    )(*seq_args, w2t, bg, cm, gout)
    return out.reshape(batch * seq, GLA_DV)


SB_TQ = 256
SB_SUBS = 2
SB_DEAD_LOG2 = 160.0
SB_NO_BLOCK = 1e30


def _sb_kernel(q_ref, k_ref, v_ref, o_ref, acc_ref, c_ref):
    step = pl.program_id(2)
    tq = SB_TQ
    lane = lax.broadcasted_iota(jnp.int32, (tq, LANES), 1)
    row = lax.broadcasted_iota(jnp.int32, (tq, tq), 0)
    col = lax.broadcasted_iota(jnp.int32, (tq, tq), 1)
    from_s = jnp.where(row >= col, 1.0, 0.0).astype(BF16)
    from_s = jnp.concatenate([from_s, from_s], axis=0)
    past = col < row

    def q_heads(u):
        q = q_ref[u * tq:(u + 1) * tq, :].astype(BF16)
        zero = jnp.zeros_like(q)
        return (jnp.where(lane < SB_HEAD_DIM, q, zero), jnp.where(lane >= SB_HEAD_DIM, q, zero))

    def load(kb):
        start = pl.multiple_of(kb * tq, tq)
        return k_ref[pl.ds(start, tq), :].astype(BF16), v_ref[pl.ds(start, tq), :].astype(BF16)

    def neg_log2_keep(qh, kblk):
        y = lax.dot_general(qh, kblk, NT_DIMS, preferred_element_type=F32)
        return y, jnp.maximum(y, 0.0) + jnp.log2(1.0 + jnp.exp2(-jnp.abs(y)))

    def weights(y, p, carry):
        total = jnp.dot(jnp.concatenate(_split_bf16(p), axis=1), from_s, preferred_element_type=F32)
        if carry is not None:
            total = total + carry
        return jnp.exp2(y - total)

    def pv(w, vblk):
        return jnp.dot(w.astype(BF16), vblk, preferred_element_type=F32)

    first = step * SB_SUBS
    blocks = [load(jnp.maximum(first - 1, 0))] + [load(first + u) for u in range(SB_SUBS)]
    no_prev = jnp.where(first > 0, 0.0, SB_NO_BLOCK)
    chains = [(u, h) for u in range(SB_SUBS) for h in range(2)]
    qh = [q_heads(u) for u in range(SB_SUBS)]
    alive = []

    def scores(u, h):
        y, p = neg_log2_keep(qh[u][h], blocks[u + 1][0])
        return (y, jnp.where(past, p, 0.0)), neg_log2_keep(qh[u][h], blocks[u][0])

    def finish(u, h, diag, prev):
        yd, pd = diag
        yp, pp = prev
        cd = jnp.sum(pd, axis=-1, keepdims=True)
        wd = jnp.where(past, weights(yd, pd, None), 0.0)
        wp = weights(yp, pp, cd + no_prev if u == 0 else cd)
        acc_ref[u, h] = pv(wd, blocks[u + 1][1]) + pv(wp, blocks[u][1])
        c = cd + jnp.sum(pp, axis=-1, keepdims=True)
        c_ref[u, h] = jnp.broadcast_to(c, (tq, LANES))
        alive.append(jnp.min(c))

    pending = scores(*chains[0])
    for i, (u, h) in enumerate(chains):
        upcoming = scores(*chains[i + 1]) if i + 1 < len(chains) else None
        finish(u, h, *pending)
        pending = upcoming

    def cond(state):
        kb, go = state
        return jnp.logical_and(kb >= 0, go > 0)

    for u in range(SB_SUBS):
        def body(state, u=u):
            kb, _ = state
            kblk, vblk = load(kb)
            cmin = None
            for h in range(2):
                y, p = neg_log2_keep(qh[u][h], kblk)
                c = c_ref[u, h]
                w = weights(y, p, jnp.concatenate([c, c], axis=1))
                acc_ref[u, h] += pv(w, vblk)
                c = c + jnp.sum(p, axis=-1, keepdims=True)
                c_ref[u, h] = c
                m = jnp.min(c)
                cmin = m if cmin is None else jnp.minimum(cmin, m)
            return kb - 1, (cmin < SB_DEAD_LOG2).astype(jnp.int32)

        carry_min = jnp.minimum(alive[2 * u], alive[2 * u + 1])
        lax.while_loop(cond, body, (first + u - 2, (carry_min < SB_DEAD_LOG2).astype(jnp.int32)))
        o_ref[u * tq:(u + 1) * tq, :] = jnp.where(
            lane < SB_HEAD_DIM, acc_ref[u, 0], acc_ref[u, 1]).astype(o_ref.dtype)


def _sb_attention(qn, kn, vb, batch, seq):
    rows = SB_SUBS * SB_TQ
    pairs = SB_D // LANES
    q3, k3, v3 = (a.reshape(batch, seq, SB_D) for a in (qn, kn, vb))
    out = pl.pallas_call(
        _sb_kernel,
        grid=(batch, pairs, seq // rows),
        in_specs=[
            pl.BlockSpec((None, rows, LANES), lambda b, p, i: (b, i, p)),
            pl.BlockSpec((None, seq, LANES), lambda b, p, i: (b, 0, p)),
            pl.BlockSpec((None, seq, LANES), lambda b, p, i: (b, 0, p)),
        ],
        out_specs=pl.BlockSpec((None, rows, LANES), lambda b, p, i: (b, i, p)),
        out_shape=jax.ShapeDtypeStruct((batch, seq, SB_D), ACT_DTYPE),
        scratch_shapes=[pltpu.VMEM((SB_SUBS, 2, SB_TQ, LANES), F32),
                        pltpu.VMEM((SB_SUBS, 2, SB_TQ, LANES), F32)],
        compiler_params=_cparams(("parallel", "parallel", "arbitrary"), 32),
        name="sb_attention",
    )(q3, k3, v3)
    return out.reshape(batch * seq, SB_D)


FF_CHUNK = 256


def _ffn_tail(h1, g_ref, wg_ref, wu_ref, wd_ref, out_ref, acc_ref):
    hb = _rms(h1, g_ref[...]).astype(BF16)
    acc_ref[...] = h1
    for c in range(D_FF // FF_CHUNK):
        cols = slice(c * FF_CHUNK, (c + 1) * FF_CHUNK)
        a = jnp.dot(hb, wg_ref[:, cols], preferred_element_type=F32)
        u = jnp.dot(hb, wu_ref[:, cols], preferred_element_type=F32)
        act = (a * _sigmoid(a) * u).astype(BF16)
        acc_ref[...] += jnp.dot(act, wd_ref[cols, :], preferred_element_type=F32)
    out_ref[...] = acc_ref[...]


def _outproj_ffn_kernel(h_ref, og_ref, os_ref, wo_ref, g_ref, wg_ref, wu_ref, wd_ref,
                        out_ref, acc_ref):
    mix = (jnp.dot(og_ref[...].astype(BF16), wo_ref[:GLA_DV, :], preferred_element_type=F32)
           + jnp.dot(os_ref[...].astype(BF16), wo_ref[GLA_DV:, :], preferred_element_type=F32))
    _ffn_tail(h_ref[...] + mix, g_ref, wg_ref, wu_ref, wd_ref, out_ref, acc_ref)


def _ffn_specs():
    return [
        _const_spec((1, D_MODEL)),
        _const_spec((D_MODEL, D_FF)),
        _const_spec((D_MODEL, D_FF)),
        _const_spec((D_FF, D_MODEL)),
    ]


def _outproj_ffn(h2d, o_gla, o_sb, wo, g, wg, wu, wd, tm=512):
    m = h2d.shape[0]
    row = pl.BlockSpec((tm, D_MODEL), lambda i: (i, 0))
    half = pl.BlockSpec((tm, GLA_DV), lambda i: (i, 0))
    return pl.pallas_call(
        _outproj_ffn_kernel,
        grid=(m // tm,),
        in_specs=[row, half, half, _const_spec((D_MODEL, D_MODEL))] + _ffn_specs(),
        out_specs=row,
        out_shape=jax.ShapeDtypeStruct((m, D_MODEL), F32),
        scratch_shapes=[pltpu.VMEM((tm, D_MODEL), F32)],
        compiler_params=_cparams(("parallel",), 56),
        name="outproj_ffn",
    )(h2d, o_gla, o_sb, wo, g, wg, wu, wd)


HALO = 32
CONV_ROWS = 128
CONV_COLS = 256


def _interleave(first, second):
    lists = (first, second)
    total = [sum(cost for cost, _ in lst) for lst in lists]
    done, idx = [0.0, 0.0], [0, 0]
    while idx[0] < len(first) or idx[1] < len(second):
        if idx[1] >= len(second):
            k = 0
        elif idx[0] >= len(first):
            k = 1
        else:
            k = 0 if done[0] / total[0] <= done[1] / total[1] else 1
        cost, thunk = lists[k][idx[k]]
        thunk()
        done[k] += cost
        idx[k] += 1


def _layer1_kernel(h_ref, g_ref, w1_ref, b1_ref, wdw_ref, bdw_ref, lng_ref, lnb_ref,
                   w2_ref, b2_ref, gf_ref, wg_ref, wu_ref, wd_ref, out_ref,
                   u_ref, y_ref, sh_ref, mid_ref, acc_ref, *, tiles_per_seq):
    s = pl.program_id(0)
    tc = h_ref.shape[0]

    @pl.when(s == 0)
    def _():
        mid_ref[...] = jnp.zeros_like(mid_ref)

    @pl.when(s % tiles_per_seq == 0)
    def _():
        u_ref[:HALO, :] = jnp.zeros((HALO, D_MODEL), F32)

    prev = mid_ref[(s + 1) % 2]
    hb = _rms(prev, gf_ref[...]).astype(BF16)
    acc_ref[...] = prev

    def ffn_chunk(c):
        cols = slice(c * FF_CHUNK, (c + 1) * FF_CHUNK)
        a = jnp.dot(hb, wg_ref[:, cols], preferred_element_type=F32)
        u = jnp.dot(hb, wu_ref[:, cols], preferred_element_type=F32)
        act = (a * _sigmoid(a) * u).astype(BF16)
        acc_ref[...] += jnp.dot(act, wd_ref[cols, :], preferred_element_type=F32)

    ffn_pieces = [(1270, functools.partial(ffn_chunk, c)) for c in range(D_FF // FF_CHUNK)]

    xb = _rms(h_ref[...], g_ref[...]).astype(BF16)
    shift = HALO - (CONV_WIDTH - 1)
    span = HALO + tc

    def glu(c):
        cols = slice(c * CONV_COLS, (c + 1) * CONV_COLS)
        gcols = slice(D_MODEL + c * CONV_COLS, D_MODEL + (c + 1) * CONV_COLS)
        lin = jnp.dot(xb, w1_ref[:, cols], preferred_element_type=F32) + b1_ref[:, cols]
        gate = jnp.dot(xb, w1_ref[:, gcols], preferred_element_type=F32) + b1_ref[:, gcols]
        u_ref[HALO:, cols] = lin * _sigmoid(gate)

    def shifted_copies(c):
        slab = u_ref[:, c * CONV_COLS:(c + 1) * CONV_COLS]
        for r in range(1, SUBLANES):
            sh_ref[r - 1] = pltpu.roll(slab, span - r, axis=0)

    def conv(c, rb):
        cols = slice(c * CONV_COLS, (c + 1) * CONV_COLS)
        r0 = rb * CONV_ROWS
        acc = jnp.broadcast_to(bdw_ref[:, cols], (CONV_ROWS, CONV_COLS))
        for j in range(CONV_WIDTH):
            a, r = divmod(shift + j, SUBLANES)
            rows = slice(r0 + SUBLANES * a, r0 + SUBLANES * a + CONV_ROWS)
            tap = u_ref[rows, cols] if r == 0 else sh_ref[r - 1, rows, :]
            acc = acc + wdw_ref[j:j + 1, cols] * tap
        y_ref[r0:r0 + CONV_ROWS, cols] = acc

    def carry_halo():
        u_ref[:HALO, :] = u_ref[tc:tc + HALO, :]

    def norm_project(rb):
        rows = slice(rb * CONV_ROWS, (rb + 1) * CONV_ROWS)
        y = y_ref[rows, :]
        mu = jnp.mean(y, axis=-1, keepdims=True)
        d = y - mu
        var = jnp.mean(d * d, axis=-1, keepdims=True)
        yn = d * lax.rsqrt(var + EPS) * lng_ref[...] + lnb_ref[...]
        act = (yn * _sigmoid(yn)).astype(BF16)
        mid_ref[s % 2, rows, :] = (h_ref[rows, :] + b2_ref[...]
                                   + jnp.dot(act, w2_ref[...], preferred_element_type=F32))

    conf_pieces = []
    for c in range(D_MODEL // CONV_COLS):
        conf_pieces.append((270, functools.partial(glu, c)))
        conf_pieces.append((800, functools.partial(shifted_copies, c)))
        conf_pieces += [(520, functools.partial(conv, c, rb)) for rb in range(tc // CONV_ROWS)]
    conf_pieces.append((10, carry_halo))
    conf_pieces += [(750, functools.partial(norm_project, rb)) for rb in range(tc // CONV_ROWS)]

    _interleave(conf_pieces, ffn_pieces)
    out_ref[...] = acc_ref[...]


def _layer1(h2d, g, w1, b1, wdw, bdw, lng, lnb, w2, b2, gf, wg, wu, wd, seq, tc=512):
    m = h2d.shape[0]
    tiles = m // tc
    kernel = functools.partial(_layer1_kernel, tiles_per_seq=seq // tc)
    return pl.pallas_call(
        kernel,
        grid=(tiles + 1,),
        in_specs=[
            pl.BlockSpec((tc, D_MODEL), lambda s: (jnp.minimum(s, tiles - 1), 0)),
            _const_spec((1, D_MODEL)),
            _const_spec((D_MODEL, 2 * D_MODEL)),
            _const_spec((1, 2 * D_MODEL)),
            _const_spec((HALO, D_MODEL)),
            _const_spec((1, D_MODEL)),
            _const_spec((1, D_MODEL)),
            _const_spec((1, D_MODEL)),
            _const_spec((D_MODEL, D_MODEL)),
            _const_spec((1, D_MODEL)),
        ] + _ffn_specs(),
        out_specs=pl.BlockSpec((tc, D_MODEL), lambda s: (jnp.maximum(s - 1, 0), 0)),
        out_shape=jax.ShapeDtypeStruct((m, D_MODEL), F32),
        scratch_shapes=[
            pltpu.VMEM((HALO + tc, D_MODEL), F32),
            pltpu.VMEM((tc, D_MODEL), F32),
            pltpu.VMEM((SUBLANES - 1, HALO + tc, CONV_COLS), F32),
            pltpu.VMEM((2, tc, D_MODEL), F32),
            pltpu.VMEM((tc, D_MODEL), F32),
        ],
        compiler_params=_cparams(("arbitrary",), 58),
        name="layer1",
    )(h2d, g, w1, b1, wdw, bdw, lng, lnb, w2, b2, gf, wg, wu, wd)


def kernel(x, mix_norm, ffn_norm, hy_w_in, hy_w_gate2, hy_b_gate, hy_gla_norm, hy_sb_q_norm,
           hy_sb_k_norm, hy_w_out, cv_w_pw1, cv_b_pw1, cv_w_dw, cv_b_dw, cv_ln_g, cv_ln_b,
           cv_w_pw2, cv_b_pw2, ffn_w_gate, ffn_w_up, ffn_w_down):
    batch, seq, _ = x.shape
    m = batch * seq
    h = x.reshape(m, D_MODEL)

    w_in = hy_w_in[0]
    cuts = np.cumsum([0, GLA_DK, GLA_DK, GLA_DV, GLA_DV, GLA_GATE_RANK, SB_D, SB_D, SB_D])
    w_gq, w_gk, w_gv, w_gr, w_glr, w_sq, w_sk, w_sv = (
        w_in[:, cuts[i]:cuts[i + 1]] for i in range(8))
    w_row = jnp.concatenate([w_gv, w_gr, w_gq], axis=1).astype(BF16)
    w_sb = w_in[:, cuts[5]:cuts[8]].astype(BF16)
    w_kt = w_gk.T.astype(BF16)
    w_gt = jnp.pad(w_glr.T, ((0, LANES - GLA_GATE_RANK), (0, 0))).astype(BF16)
    gq2 = jnp.tile(hy_sb_q_norm[0], 2)[None, :]
    gk2 = jnp.tile(hy_sb_k_norm[0], 2)[None, :]
    proj, qn, kn, vb, kt, glrt = _inproj(h, mix_norm[0][None, :], w_row, w_sb, w_kt, w_gt, gq2, gk2)

    w2t = jnp.pad(hy_w_gate2[0].T, ((0, 0), (0, LANES - GLA_GATE_RANK))).astype(BF16)
    bg = jnp.broadcast_to(hy_b_gate[0][:, None], (GLA_DK, LANES)).astype(F32)
    cm = jnp.asarray(_gla_cumsum_matrix(), dtype=BF16)
    o_gla = _gla(proj, kt, glrt, w2t, bg, cm, hy_gla_norm[0][None, :], batch, seq)

    o_sb = _sb_attention(qn, kn, vb, batch, seq)

    h = _outproj_ffn(h, o_gla, o_sb, hy_w_out[0].astype(BF16), ffn_norm[0][None, :],
                     ffn_w_gate[0].astype(BF16), ffn_w_up[0].astype(BF16),
                     ffn_w_down[0].astype(BF16))

    wdw = jnp.pad(cv_w_dw[0], ((0, HALO - CONV_WIDTH), (0, 0)))
    h = _layer1(h, mix_norm[1][None, :], cv_w_pw1[0].astype(BF16), cv_b_pw1[0][None, :],
                wdw, cv_b_dw[0][None, :], cv_ln_g[0][None, :], cv_ln_b[0][None, :],
                cv_w_pw2[0].astype(BF16), cv_b_pw2[0][None, :], ffn_norm[1][None, :],
                ffn_w_gate[1].astype(BF16), ffn_w_up[1].astype(BF16), ffn_w_down[1].astype(BF16), seq)
    return h.reshape(batch, seq, D_MODEL)
```

```python
import functools

import numpy as np
import jax
import jax.numpy as jnp
from jax import lax
from jax.experimental import pallas as pl
from jax.experimental.pallas import tpu as pltpu

F32 = jnp.float32
BF16 = jnp.bfloat16
ACT_DTYPE = BF16

D_MODEL = 1024
CHUNK = 64
EPS = 1e-6
GLA_HEADS = 4
GLA_HEAD_K = 64
GLA_HEAD_V = 128
GLA_DK = GLA_HEADS * GLA_HEAD_K
GLA_DV = GLA_HEADS * GLA_HEAD_V
GLA_GATE_RANK = 16
GLA_GATE_NORMALIZER = 16.0
SB_HEADS = 8
SB_HEAD_DIM = 64
SB_D = SB_HEADS * SB_HEAD_DIM
CONV_WIDTH = 31
D_FF = 2816

LOG2_E = 1.4426950408889634
LANES = 128
SUBLANES = 8
MIB = 1024 * 1024
PROJ_W = GLA_DV + GLA_DV + GLA_DK
COL_GV, COL_GR, COL_GQ = 0, 512, 1024

NT_DIMS = (((1,), (1,)), ((), ()))


def _cparams(sem, vmem_mib):
    return pltpu.CompilerParams(dimension_semantics=sem, vmem_limit_bytes=vmem_mib * MIB)


def _rms(x, g):
    return x * lax.rsqrt(jnp.mean(x * x, axis=-1, keepdims=True) + EPS) * g


def _sigmoid(x):
    return 1.0 / (1.0 + jnp.exp(-x))


def _log_sigmoid(x):
    return jnp.minimum(x, 0.0) - jnp.log(1.0 + jnp.exp(-jnp.abs(x)))


def _split_bf16(x):
    hi = x.astype(BF16)
    lo = (x - hi.astype(F32)).astype(BF16)
    return hi, lo


def _const_spec(shape):
    nd = len(shape)
    return pl.BlockSpec(shape, lambda *_: (0,) * nd, pipeline_mode=pl.Buffered(1))


def _head_pair_norm(x, g, first):
    x2 = x * x
    s0 = jnp.sum(jnp.where(first, x2, 0.0), axis=-1, keepdims=True)
    s1 = jnp.sum(jnp.where(first, 0.0, x2), axis=-1, keepdims=True)
    ms = jnp.where(first, s0, s1) * (1.0 / SB_HEAD_DIM)
    return x * lax.rsqrt(ms + EPS) * g


def _inproj_kernel(x_ref, g_ref, w_ref, wsb_ref, wkt_ref, wgt_ref, gq_ref, gk_ref,
                   proj_ref, qn_ref, kn_ref, vb_ref, kt_ref, glrt_ref):
    tm = x_ref.shape[0]
    xb = _rms(x_ref[...], g_ref[...]).astype(BF16)
    proj_ref[...] = jnp.dot(xb, w_ref[...], preferred_element_type=F32)
    kt_ref[...] = lax.dot_general(wkt_ref[...], xb, NT_DIMS, preferred_element_type=F32)
    glrt_ref[...] = lax.dot_general(wgt_ref[...], xb, NT_DIMS, preferred_element_type=F32)
    first = lax.broadcasted_iota(jnp.int32, (tm, LANES), 1) < SB_HEAD_DIM
    q_scale = SB_HEAD_DIM ** -0.5 * LOG2_E
    wide = 2 * LANES
    for c in range(SB_D // wide):
        sq = jnp.dot(xb, wsb_ref[:, c * wide:(c + 1) * wide], preferred_element_type=F32)
        sk = jnp.dot(xb, wsb_ref[:, SB_D + c * wide:SB_D + (c + 1) * wide],
                     preferred_element_type=F32)
        for p in range(2):
            src = slice(p * LANES, (p + 1) * LANES)
            dst = slice(c * wide + p * LANES, c * wide + (p + 1) * LANES)
            qn = _head_pair_norm(sq[:, src], gq_ref[...], first) * q_scale
            qn_ref[:, dst] = qn.astype(qn_ref.dtype)
            kn_ref[:, dst] = _head_pair_norm(sk[:, src], gk_ref[...], first).astype(kn_ref.dtype)
    vb_ref[...] = jnp.dot(xb, wsb_ref[:, 2 * SB_D:], preferred_element_type=F32).astype(vb_ref.dtype)


def _inproj(x2d, g, w, wsb, wkt, wgt, gq2, gk2, tm=512):
    m = x2d.shape[0]
    sb_spec = pl.BlockSpec((tm, SB_D), lambda i: (i, 0))
    sb_shape = jax.ShapeDtypeStruct((m, SB_D), ACT_DTYPE)
    return pl.pallas_call(
        _inproj_kernel,
        grid=(m // tm,),
        in_specs=[
            pl.BlockSpec((tm, D_MODEL), lambda i: (i, 0)),
            _const_spec((1, D_MODEL)),
            _const_spec((D_MODEL, PROJ_W)),
            _const_spec((D_MODEL, 3 * SB_D)),
            _const_spec((GLA_DK, D_MODEL)),
            _const_spec((LANES, D_MODEL)),
            _const_spec((1, LANES)),
            _const_spec((1, LANES)),
        ],
        out_specs=[
            pl.BlockSpec((tm, PROJ_W), lambda i: (i, 0)),
            sb_spec, sb_spec, sb_spec,
            pl.BlockSpec((GLA_DK, tm), lambda i: (0, i)),
            pl.BlockSpec((LANES, tm), lambda i: (0, i)),
        ],
        out_shape=[
            jax.ShapeDtypeStruct((m, PROJ_W), F32),
            sb_shape, sb_shape, sb_shape,
            jax.ShapeDtypeStruct((GLA_DK, m), F32),
            jax.ShapeDtypeStruct((LANES, m), F32),
        ],
        compiler_params=_cparams(("parallel",), 48),
        name="inproj",
    )(x2d, g, w, wsb, wkt, wgt, gq2, gk2)


GLA_TILE = 2 * CHUNK


def _gla_cumsum_matrix():
    r = np.arange(GLA_TILE)[:, None]
    i = np.arange(GLA_TILE)[None, :]
    same = (r // CHUNK) == (i // CHUNK)
    rem = (r > i) & same
    tot0 = np.broadcast_to(r < CHUNK, (GLA_TILE, GLA_TILE))
    tot1 = np.broadcast_to(r >= CHUNK, (GLA_TILE, GLA_TILE))
    return np.concatenate([rem, tot0, tot1], axis=1).astype(np.float32)


def _gla_kernel(*refs, tiles, batch):
    seq_refs = [refs[5 * b:5 * b + 5] for b in range(batch)]
    w2t_ref, bg_ref, cm_ref, gout_ref, o_ref, s_ref = refs[5 * batch:]

    @pl.when(pl.program_id(0) == 0)
    def _():
        s_ref[...] = jnp.zeros_like(s_ref)

    lane = lax.broadcasted_iota(jnp.int32, (GLA_DK, GLA_TILE), 1)
    for t in range(tiles):
        cols = slice(t * GLA_TILE, (t + 1) * GLA_TILE)
        k_half, decay, v = [], [], []
        for b in range(batch):
            _, v_ref, _, kt_ref, glrt_ref = seq_refs[b]
            glr = glrt_ref[:, cols].astype(BF16)
            pre = jnp.dot(w2t_ref[...], glr, preferred_element_type=F32) + bg_ref[...]
            la = _log_sigmoid(pre) * (1.0 / GLA_GATE_NORMALIZER)
            hi, lo = _split_bf16(la)
            sums = (jnp.dot(hi, cm_ref[...], preferred_element_type=F32)
                    + jnp.dot(lo, cm_ref[...], preferred_element_type=F32))
            k_end = (kt_ref[:, cols] * jnp.exp(sums[:, :GLA_TILE])).astype(BF16)
            decay.append((jnp.exp(sums[:, GLA_TILE:2 * GLA_TILE]),
                          jnp.exp(sums[:, 2 * GLA_TILE:])))
            k_half.append((jnp.where(lane < CHUNK, k_end, jnp.zeros_like(k_end)),
                           jnp.where(lane >= CHUNK, k_end, jnp.zeros_like(k_end))))
            v.append(v_ref[cols, :].astype(BF16))
        for j in range(2):
            rows = slice(t * GLA_TILE + j * CHUNK, t * GLA_TILE + (j + 1) * CHUNK)
            for b in range(batch):
                q_ref, _, r_ref, _, _ = seq_refs[b]
                for h in range(GLA_HEADS):
                    kr = slice(h * GLA_HEAD_K, (h + 1) * GLA_HEAD_K)
                    vc = slice(h * GLA_HEAD_V, (h + 1) * GLA_HEAD_V)
                    upd = jnp.dot(k_half[b][j][kr, :], v[b][:, vc], preferred_element_type=F32)
                    s_ref[b, kr, vc] = decay[b][j][kr, :] * s_ref[b, kr, vc] + upd
                q = q_ref[rows, :].astype(BF16)
                o = jnp.dot(q, s_ref[b].astype(BF16), preferred_element_type=F32)
                o = o * (GLA_HEAD_K ** -0.5)
                gate = r_ref[rows, :]
                for h in range(GLA_HEADS):
                    vc = slice(h * GLA_HEAD_V, (h + 1) * GLA_HEAD_V)
                    oh = _rms(o[:, vc], gout_ref[...])
                    g = gate[:, vc]
                    o_ref[b, rows, vc] = (oh * (g * _sigmoid(g))).astype(o_ref.dtype)


def _gla(proj, kt, glrt, w2t, bg, cm, gout, batch, seq, tr=512):
    nt = seq // tr
    kernel = functools.partial(_gla_kernel, tiles=tr // GLA_TILE, batch=batch)
    seq_specs, seq_args = [], []
    for b in range(batch):
        row = lambda i, b=b: b * nt + i
        seq_specs += [
            pl.BlockSpec((tr, GLA_DK), lambda i, row=row: (row(i), COL_GQ // GLA_DK)),
            pl.BlockSpec((tr, GLA_DV), lambda i, row=row: (row(i), COL_GV // GLA_DV)),
            pl.BlockSpec((tr, GLA_DV), lambda i, row=row: (row(i), COL_GR // GLA_DV)),
            pl.BlockSpec((GLA_DK, tr), lambda i, row=row: (0, row(i))),
            pl.BlockSpec((LANES, tr), lambda i, row=row: (0, row(i))),
        ]
        seq_args += [proj, proj, proj, kt, glrt]
    out = pl.pallas_call(
        kernel,
        grid=(nt,),
        in_specs=seq_specs + [
            _const_spec((GLA_DK, LANES)),
            _const_spec((GLA_DK, LANES)),
            _const_spec((GLA_TILE, 3 * GLA_TILE)),
            _const_spec((1, GLA_HEAD_V)),
        ],
        out_specs=pl.BlockSpec((batch, tr, GLA_DV), lambda i: (0, i, 0)),
        out_shape=jax.ShapeDtypeStruct((batch, seq, GLA_DV), ACT_DTYPE),
        scratch_shapes=[pltpu.VMEM((batch, GLA_DK, GLA_DV), F32)],
        compiler_params=_cparams(("arbitrary",), 32),
        name="gla",
    )(*seq_args, w2t, bg, cm, gout)
    return out.reshape(batch * seq, GLA_DV)


SB_TQ = 256
SB_SUBS = 4
SB_DEAD_LOG2 = 160.0
SB_NO_BLOCK = 1e30


def _sb_kernel(q_ref, k_ref, v_ref, o_ref, acc_ref, c_ref):
    step = pl.program_id(2)
    tq = SB_TQ
    lane = lax.broadcasted_iota(jnp.int32, (tq, LANES), 1)
    row = lax.broadcasted_iota(jnp.int32, (tq, tq), 0)
    col = lax.broadcasted_iota(jnp.int32, (tq, tq), 1)
    from_s = jnp.where(row >= col, 1.0, 0.0).astype(BF16)
    from_s = jnp.concatenate([from_s, from_s], axis=0)
    past = col < row

    def q_heads(u):
        q = q_ref[u * tq:(u + 1) * tq, :].astype(BF16)
        zero = jnp.zeros_like(q)
        return (jnp.where(lane < SB_HEAD_DIM, q, zero), jnp.where(lane >= SB_HEAD_DIM, q, zero))

    def load(kb):
        start = pl.multiple_of(kb * tq, tq)
        return k_ref[pl.ds(start, tq), :].astype(BF16), v_ref[pl.ds(start, tq), :].astype(BF16)

    def neg_log2_keep(qh, kblk):
        y = lax.dot_general(qh, kblk, NT_DIMS, preferred_element_type=F32)
        return y, jnp.maximum(y, 0.0) + jnp.log2(1.0 + jnp.exp2(-jnp.abs(y)))

    def weights(y, p, carry):
        total = jnp.dot(jnp.concatenate(_split_bf16(p), axis=1), from_s, preferred_element_type=F32)
        if carry is not None:
            total = total + carry
        return jnp.exp2(y - total)

    def pv(w, vblk):
        return jnp.dot(w.astype(BF16), vblk, preferred_element_type=F32)

    first = step * SB_SUBS
    blocks = [load(jnp.maximum(first - 1, 0))] + [load(first + u) for u in range(SB_SUBS)]
    no_prev = jnp.where(first > 0, 0.0, SB_NO_BLOCK)
    chains = [(u, h) for u in range(SB_SUBS) for h in range(2)]
    qh = [q_heads(u) for u in range(SB_SUBS)]
    alive = []

    def scores(u, h):
        y, p = neg_log2_keep(qh[u][h], blocks[u + 1][0])
        return (y, jnp.where(past, p, 0.0)), neg_log2_keep(qh[u][h], blocks[u][0])

    def finish(u, h, diag, prev):
        yd, pd = diag
        yp, pp = prev
        cd = jnp.sum(pd, axis=-1, keepdims=True)
        wd = jnp.where(past, weights(yd, pd, None), 0.0)
        wp = weights(yp, pp, cd + no_prev if u == 0 else cd)
        acc_ref[u, h] = pv(wd, blocks[u + 1][1]) + pv(wp, blocks[u][1])
        c = cd + jnp.sum(pp, axis=-1, keepdims=True)
        c_ref[u, h] = jnp.broadcast_to(c, (tq, LANES))
        alive.append(jnp.min(c))

    pending = scores(*chains[0])
    for i, (u, h) in enumerate(chains):
        upcoming = scores(*chains[i + 1]) if i + 1 < len(chains) else None
        finish(u, h, *pending)
        pending = upcoming

    def cond(state):
        kb, go = state
        return jnp.logical_and(kb >= 0, go > 0)

    for u in range(SB_SUBS):
        def body(state, u=u):
            kb, _ = state
            kblk, vblk = load(kb)
            cmin = None
            for h in range(2):
                y, p = neg_log2_keep(qh[u][h], kblk)
                c = c_ref[u, h]
                w = weights(y, p, jnp.concatenate([c, c], axis=1))
                acc_ref[u, h] += pv(w, vblk)
                c = c + jnp.sum(p, axis=-1, keepdims=True)
                c_ref[u, h] = c
                m = jnp.min(c)
                cmin = m if cmin is None else jnp.minimum(cmin, m)
            return kb - 1, (cmin < SB_DEAD_LOG2).astype(jnp.int32)

        carry_min = jnp.minimum(alive[2 * u], alive[2 * u + 1])
        lax.while_loop(cond, body, (first + u - 2, (carry_min < SB_DEAD_LOG2).astype(jnp.int32)))
        o_ref[u * tq:(u + 1) * tq, :] = jnp.where(
            lane < SB_HEAD_DIM, acc_ref[u, 0], acc_ref[u, 1]).astype(o_ref.dtype)


def _sb_attention(qn, kn, vb, batch, seq):
    rows = SB_SUBS * SB_TQ
    pairs = SB_D // LANES
    q3, k3, v3 = (a.reshape(batch, seq, SB_D) for a in (qn, kn, vb))
    out = pl.pallas_call(
        _sb_kernel,
        grid=(batch, pairs, seq // rows),
        in_specs=[
            pl.BlockSpec((None, rows, LANES), lambda b, p, i: (b, i, p)),
            pl.BlockSpec((None, seq, LANES), lambda b, p, i: (b, 0, p)),
            pl.BlockSpec((None, seq, LANES), lambda b, p, i: (b, 0, p)),
        ],
        out_specs=pl.BlockSpec((None, rows, LANES), lambda b, p, i: (b, i, p)),
        out_shape=jax.ShapeDtypeStruct((batch, seq, SB_D), ACT_DTYPE),
        scratch_shapes=[pltpu.VMEM((SB_SUBS, 2, SB_TQ, LANES), F32),
                        pltpu.VMEM((SB_SUBS, 2, SB_TQ, LANES), F32)],
        compiler_params=_cparams(("parallel", "parallel", "arbitrary"), 32),
        name="sb_attention",
    )(q3, k3, v3)
    return out.reshape(batch * seq, SB_D)


FF_CHUNK = 256


def _ffn_tail(h1, g_ref, wg_ref, wu_ref, wd_ref, out_ref, acc_ref):
    hb = _rms(h1, g_ref[...]).astype(BF16)
    acc_ref[...] = h1
    for c in range(D_FF // FF_CHUNK):
        cols = slice(c * FF_CHUNK, (c + 1) * FF_CHUNK)
        a = jnp.dot(hb, wg_ref[:, cols].astype(BF16), preferred_element_type=F32)
        u = jnp.dot(hb, wu_ref[:, cols].astype(BF16), preferred_element_type=F32)
        act = (a * _sigmoid(a) * u).astype(BF16)
        acc_ref[...] += jnp.dot(act, wd_ref[cols, :].astype(BF16), preferred_element_type=F32)
    out_ref[...] = acc_ref[...]


def _outproj_ffn_kernel(h_ref, og_ref, os_ref, wo_ref, g_ref, wg_ref, wu_ref, wd_ref,
                        out_ref, acc_ref):
    mix = (jnp.dot(og_ref[...].astype(BF16), wo_ref[:GLA_DV, :], preferred_element_type=F32)
           + jnp.dot(os_ref[...].astype(BF16), wo_ref[GLA_DV:, :], preferred_element_type=F32))
    _ffn_tail(h_ref[...] + mix, g_ref, wg_ref, wu_ref, wd_ref, out_ref, acc_ref)


def _ffn_specs():
    return [
        _const_spec((1, D_MODEL)),
        _const_spec((D_MODEL, D_FF)),
        _const_spec((D_MODEL, D_FF)),
        _const_spec((D_FF, D_MODEL)),
    ]


def _outproj_ffn(h2d, o_gla, o_sb, wo, g, wg, wu, wd, tm=512):
    m = h2d.shape[0]
    row = pl.BlockSpec((tm, D_MODEL), lambda i: (i, 0))
    half = pl.BlockSpec((tm, GLA_DV), lambda i: (i, 0))
    return pl.pallas_call(
        _outproj_ffn_kernel,
        grid=(m // tm,),
        in_specs=[row, half, half, _const_spec((D_MODEL, D_MODEL))] + _ffn_specs(),
        out_specs=row,
        out_shape=jax.ShapeDtypeStruct((m, D_MODEL), F32),
        scratch_shapes=[pltpu.VMEM((tm, D_MODEL), F32)],
        compiler_params=_cparams(("parallel",), 56),
        name="outproj_ffn",
    )(h2d, o_gla, o_sb, wo, g, wg, wu, wd)


HALO = 32
CONV_ROWS = 128
CONV_COLS = 256


def _interleave(first, second):
    lists = (first, second)
    total = [sum(cost for cost, _ in lst) for lst in lists]
    done, idx = [0.0, 0.0], [0, 0]
    while idx[0] < len(first) or idx[1] < len(second):
        if idx[1] >= len(second):
            k = 0
        elif idx[0] >= len(first):
            k = 1
        else:
            k = 0 if done[0] / total[0] <= done[1] / total[1] else 1
        cost, thunk = lists[k][idx[k]]
        thunk()
        done[k] += cost
        idx[k] += 1


def _layer1_kernel(h_ref, g_ref, w1_ref, b1_ref, wdw_ref, bdw_ref, lng_ref, lnb_ref,
                   w2_ref, b2_ref, gf_ref, wg_ref, wu_ref, wd_ref, out_ref,
                   u_ref, y_ref, sh_ref, mid_ref, acc_ref, *, tiles_per_seq):
    s = pl.program_id(0)
    tc = h_ref.shape[0]

    @pl.when(s == 0)
    def _():
        mid_ref[...] = jnp.zeros_like(mid_ref)

    @pl.when(s % tiles_per_seq == 0)
    def _():
        u_ref[:HALO, :] = jnp.zeros((HALO, D_MODEL), F32)

    prev = mid_ref[(s + 1) % 2]
    hb = _rms(prev, gf_ref[...]).astype(BF16)
    acc_ref[...] = prev

    def ffn_chunk(c):
        cols = slice(c * FF_CHUNK, (c + 1) * FF_CHUNK)
        a = jnp.dot(hb, wg_ref[:, cols], preferred_element_type=F32)
        u = jnp.dot(hb, wu_ref[:, cols], preferred_element_type=F32)
        act = (a * _sigmoid(a) * u).astype(BF16)
        acc_ref[...] += jnp.dot(act, wd_ref[cols, :], preferred_element_type=F32)

    ffn_pieces = [(1270, functools.partial(ffn_chunk, c)) for c in range(D_FF // FF_CHUNK)]

    xb = _rms(h_ref[...], g_ref[...]).astype(BF16)
    shift = HALO - (CONV_WIDTH - 1)
    span = HALO + tc

    def glu(c):
        cols = slice(c * CONV_COLS, (c + 1) * CONV_COLS)
        gcols = slice(D_MODEL + c * CONV_COLS, D_MODEL + (c + 1) * CONV_COLS)
        lin = jnp.dot(xb, w1_ref[:, cols], preferred_element_type=F32) + b1_ref[:, cols]
        gate = jnp.dot(xb, w1_ref[:, gcols], preferred_element_type=F32) + b1_ref[:, gcols]
        u_ref[HALO:, cols] = lin * _sigmoid(gate)

    def shifted_copies(c):
        slab = u_ref[:, c * CONV_COLS:(c + 1) * CONV_COLS]
        for r in range(1, SUBLANES):
            sh_ref[r - 1] = pltpu.roll(slab, span - r, axis=0)

    def conv(c, rb):
        cols = slice(c * CONV_COLS, (c + 1) * CONV_COLS)
        r0 = rb * CONV_ROWS
        acc = jnp.broadcast_to(bdw_ref[:, cols], (CONV_ROWS, CONV_COLS))
        for j in range(CONV_WIDTH):
            a, r = divmod(shift + j, SUBLANES)
            rows = slice(r0 + SUBLANES * a, r0 + SUBLANES * a + CONV_ROWS)
            tap = u_ref[rows, cols] if r == 0 else sh_ref[r - 1, rows, :]
            acc = acc + wdw_ref[j:j + 1, cols] * tap
        y_ref[r0:r0 + CONV_ROWS, cols] = acc

    def carry_halo():
        u_ref[:HALO, :] = u_ref[tc:tc + HALO, :]

    def norm_project(rb):
        rows = slice(rb * CONV_ROWS, (rb + 1) * CONV_ROWS)
        y = y_ref[rows, :]
        mu = jnp.mean(y, axis=-1, keepdims=True)
        d = y - mu
        var = jnp.mean(d * d, axis=-1, keepdims=True)
        yn = d * lax.rsqrt(var + EPS) * lng_ref[...] + lnb_ref[...]
        act = (yn * _sigmoid(yn)).astype(BF16)
        mid_ref[s % 2, rows, :] = (h_ref[rows, :] + b2_ref[...]
                                   + jnp.dot(act, w2_ref[...], preferred_element_type=F32))

    conf_pieces = []
    for c in range(D_MODEL // CONV_COLS):
        conf_pieces.append((270, functools.partial(glu, c)))
        conf_pieces.append((800, functools.partial(shifted_copies, c)))
        conf_pieces += [(520, functools.partial(conv, c, rb)) for rb in range(tc // CONV_ROWS)]
    conf_pieces.append((10, carry_halo))
    conf_pieces += [(750, functools.partial(norm_project, rb)) for rb in range(tc // CONV_ROWS)]

    _interleave(conf_pieces, ffn_pieces)
    out_ref[...] = acc_ref[...]


def _layer1(h2d, g, w1, b1, wdw, bdw, lng, lnb, w2, b2, gf, wg, wu, wd, seq, tc=512):
    m = h2d.shape[0]
    tiles = m // tc
    kernel = functools.partial(_layer1_kernel, tiles_per_seq=seq // tc)
    return pl.pallas_call(
        kernel,
        grid=(tiles + 1,),
        in_specs=[
            pl.BlockSpec((tc, D_MODEL), lambda s: (jnp.minimum(s, tiles - 1), 0)),
            _const_spec((1, D_MODEL)),
            _const_spec((D_MODEL, 2 * D_MODEL)),
            _const_spec((1, 2 * D_MODEL)),
            _const_spec((HALO, D_MODEL)),
            _const_spec((1, D_MODEL)),
            _const_spec((1, D_MODEL)),
            _const_spec((1, D_MODEL)),
            _const_spec((D_MODEL, D_MODEL)),
            _const_spec((1, D_MODEL)),
        ] + _ffn_specs(),
        out_specs=pl.BlockSpec((tc, D_MODEL), lambda s: (jnp.maximum(s - 1, 0), 0)),
        out_shape=jax.ShapeDtypeStruct((m, D_MODEL), F32),
        scratch_shapes=[
            pltpu.VMEM((HALO + tc, D_MODEL), F32),
            pltpu.VMEM((tc, D_MODEL), F32),
            pltpu.VMEM((SUBLANES - 1, HALO + tc, CONV_COLS), F32),
            pltpu.VMEM((2, tc, D_MODEL), F32),
            pltpu.VMEM((tc, D_MODEL), F32),
        ],
        compiler_params=_cparams(("arbitrary",), 58),
        name="layer1",
    )(h2d, g, w1, b1, wdw, bdw, lng, lnb, w2, b2, gf, wg, wu, wd)


def kernel(x, mix_norm, ffn_norm, hy_w_in, hy_w_gate2, hy_b_gate, hy_gla_norm, hy_sb_q_norm,
           hy_sb_k_norm, hy_w_out, cv_w_pw1, cv_b_pw1, cv_w_dw, cv_b_dw, cv_ln_g, cv_ln_b,
           cv_w_pw2, cv_b_pw2, ffn_w_gate, ffn_w_up, ffn_w_down):
    batch, seq, _ = x.shape
    m = batch * seq
    h = x.reshape(m, D_MODEL)

    w_in = hy_w_in[0]
    cuts = np.cumsum([0, GLA_DK, GLA_DK, GLA_DV, GLA_DV, GLA_GATE_RANK, SB_D, SB_D, SB_D])
    w_gq, w_gk, w_gv, w_gr, w_glr, w_sq, w_sk, w_sv = (
        w_in[:, cuts[i]:cuts[i + 1]] for i in range(8))
    w_row = jnp.concatenate([w_gv, w_gr, w_gq], axis=1).astype(BF16)
    w_sb = w_in[:, cuts[5]:cuts[8]].astype(BF16)
    w_kt = w_gk.T.astype(BF16)
    w_gt = jnp.pad(w_glr.T, ((0, LANES - GLA_GATE_RANK), (0, 0))).astype(BF16)
    gq2 = jnp.tile(hy_sb_q_norm[0], 2)[None, :]
    gk2 = jnp.tile(hy_sb_k_norm[0], 2)[None, :]
    proj, qn, kn, vb, kt, glrt = _inproj(h, mix_norm[0][None, :], w_row, w_sb, w_kt, w_gt, gq2, gk2)

    w2t = jnp.pad(hy_w_gate2[0].T, ((0, 0), (0, LANES - GLA_GATE_RANK))).astype(BF16)
    bg = jnp.broadcast_to(hy_b_gate[0][:, None], (GLA_DK, LANES)).astype(F32)
    cm = jnp.asarray(_gla_cumsum_matrix(), dtype=BF16)
    o_gla = _gla(proj, kt, glrt, w2t, bg, cm, hy_gla_norm[0][None, :], batch, seq)

    o_sb = _sb_attention(qn, kn, vb, batch, seq)

    h = _outproj_ffn(h, o_gla, o_sb, hy_w_out[0].astype(BF16), ffn_norm[0][None, :],
                     ffn_w_gate[0], ffn_w_up[0], ffn_w_down[0])

    wdw = jnp.pad(cv_w_dw[0], ((0, HALO - CONV_WIDTH), (0, 0)))
    h = _layer1(h, mix_norm[1][None, :], cv_w_pw1[0].astype(BF16), cv_b_pw1[0][None, :],
                wdw, cv_b_dw[0][None, :], cv_ln_g[0][None, :], cv_ln_b[0][None, :],
                cv_w_pw2[0].astype(BF16), cv_b_pw2[0][None, :], ffn_norm[1][None, :],
                ffn_w_gate[1].astype(BF16), ffn_w_up[1].astype(BF16), ffn_w_down[1].astype(BF16), seq)
    return h.reshape(batch, seq, D_MODEL)
```

```python
import functools

import numpy as np
import jax
import jax.numpy as jnp
from jax import lax
from jax.experimental import pallas as pl
from jax.experimental.pallas import tpu as pltpu

F32 = jnp.float32
BF16 = jnp.bfloat16
ACT_DTYPE = BF16

D_MODEL = 1024
CHUNK = 64
EPS = 1e-6
GLA_HEADS = 4
GLA_HEAD_K = 64
GLA_HEAD_V = 128
GLA_DK = GLA_HEADS * GLA_HEAD_K
GLA_DV = GLA_HEADS * GLA_HEAD_V
GLA_GATE_RANK = 16
GLA_GATE_NORMALIZER = 16.0
SB_HEADS = 8
SB_HEAD_DIM = 64
SB_D = SB_HEADS * SB_HEAD_DIM
CONV_WIDTH = 31
D_FF = 2816

LOG2_E = 1.4426950408889634
LANES = 128
SUBLANES = 8
MIB = 1024 * 1024
PROJ_W = GLA_DV + GLA_DV + GLA_DK
COL_GV, COL_GR, COL_GQ = 0, 512, 1024

NT_DIMS = (((1,), (1,)), ((), ()))


def _cparams(sem, vmem_mib):
    return pltpu.CompilerParams(dimension_semantics=sem, vmem_limit_bytes=vmem_mib * MIB)


def _rms(x, g):
    return x * lax.rsqrt(jnp.mean(x * x, axis=-1, keepdims=True) + EPS) * g


def _sigmoid(x):
    return 1.0 / (1.0 + jnp.exp(-x))


def _log_sigmoid(x):
    return jnp.minimum(x, 0.0) - jnp.log(1.0 + jnp.exp(-jnp.abs(x)))


def _split_bf16(x):
    hi = x.astype(BF16)
    lo = (x - hi.astype(F32)).astype(BF16)
    return hi, lo


def _const_spec(shape):
    nd = len(shape)
    return pl.BlockSpec(shape, lambda *_: (0,) * nd, pipeline_mode=pl.Buffered(1))


def _head_pair_norm(x, g, first):
    x2 = x * x
    s0 = jnp.sum(jnp.where(first, x2, 0.0), axis=-1, keepdims=True)
    s1 = jnp.sum(jnp.where(first, 0.0, x2), axis=-1, keepdims=True)
    ms = jnp.where(first, s0, s1) * (1.0 / SB_HEAD_DIM)
    return x * lax.rsqrt(ms + EPS) * g


def _inproj_kernel(x_ref, g_ref, w_ref, wsb_ref, wkt_ref, wgt_ref, gq_ref, gk_ref,
                   proj_ref, qn_ref, kn_ref, vb_ref, kt_ref, glrt_ref):
    tm = x_ref.shape[0]
    xb = _rms(x_ref[...], g_ref[...]).astype(BF16)
    proj_ref[...] = jnp.dot(xb, w_ref[...], preferred_element_type=F32)
    kt_ref[...] = lax.dot_general(wkt_ref[...], xb, NT_DIMS, preferred_element_type=F32)
    glrt_ref[...] = lax.dot_general(wgt_ref[...], xb, NT_DIMS, preferred_element_type=F32)
    first = lax.broadcasted_iota(jnp.int32, (tm, LANES), 1) < SB_HEAD_DIM
    q_scale = SB_HEAD_DIM ** -0.5 * LOG2_E
    wide = 2 * LANES
    for c in range(SB_D // wide):
        sq = jnp.dot(xb, wsb_ref[:, c * wide:(c + 1) * wide], preferred_element_type=F32)
        sk = jnp.dot(xb, wsb_ref[:, SB_D + c * wide:SB_D + (c + 1) * wide],
                     preferred_element_type=F32)
        for p in range(2):
            src = slice(p * LANES, (p + 1) * LANES)
            dst = slice(c * wide + p * LANES, c * wide + (p + 1) * LANES)
            qn = _head_pair_norm(sq[:, src], gq_ref[...], first) * q_scale
            qn_ref[:, dst] = qn.astype(qn_ref.dtype)
            kn_ref[:, dst] = _head_pair_norm(sk[:, src], gk_ref[...], first).astype(kn_ref.dtype)
    vb_ref[...] = jnp.dot(xb, wsb_ref[:, 2 * SB_D:], preferred_element_type=F32).astype(vb_ref.dtype)


def _inproj(x2d, g, w, wsb, wkt, wgt, gq2, gk2, tm=512):
    m = x2d.shape[0]
    sb_spec = pl.BlockSpec((tm, SB_D), lambda i: (i, 0))
    sb_shape = jax.ShapeDtypeStruct((m, SB_D), ACT_DTYPE)
    return pl.pallas_call(
        _inproj_kernel,
        grid=(m // tm,),
        in_specs=[
            pl.BlockSpec((tm, D_MODEL), lambda i: (i, 0)),
            _const_spec((1, D_MODEL)),
            _const_spec((D_MODEL, PROJ_W)),
            _const_spec((D_MODEL, 3 * SB_D)),
            _const_spec((GLA_DK, D_MODEL)),
            _const_spec((LANES, D_MODEL)),
            _const_spec((1, LANES)),
            _const_spec((1, LANES)),
        ],
        out_specs=[
            pl.BlockSpec((tm, PROJ_W), lambda i: (i, 0)),
            sb_spec, sb_spec, sb_spec,
            pl.BlockSpec((GLA_DK, tm), lambda i: (0, i)),
            pl.BlockSpec((LANES, tm), lambda i: (0, i)),
        ],
        out_shape=[
            jax.ShapeDtypeStruct((m, PROJ_W), F32),
            sb_shape, sb_shape, sb_shape,
            jax.ShapeDtypeStruct((GLA_DK, m), F32),
            jax.ShapeDtypeStruct((LANES, m), F32),
        ],
        compiler_params=_cparams(("parallel",), 48),
        name="inproj",
    )(x2d, g, w, wsb, wkt, wgt, gq2, gk2)


GLA_TILE = 2 * CHUNK


def _gla_cumsum_matrix():
    r = np.arange(GLA_TILE)[:, None]
    i = np.arange(GLA_TILE)[None, :]
    same = (r // CHUNK) == (i // CHUNK)
    rem = (r > i) & same
    tot0 = np.broadcast_to(r < CHUNK, (GLA_TILE, GLA_TILE))
    tot1 = np.broadcast_to(r >= CHUNK, (GLA_TILE, GLA_TILE))
    return np.concatenate([rem, tot0, tot1], axis=1).astype(np.float32)


def _gla_kernel(*refs, tiles, batch):
    seq_refs = [refs[5 * b:5 * b + 5] for b in range(batch)]
    w2t_ref, bg_ref, cm_ref, gout_ref, o_ref, s_ref = refs[5 * batch:]

    @pl.when(pl.program_id(0) == 0)
    def _():
        s_ref[...] = jnp.zeros_like(s_ref)

    lane = lax.broadcasted_iota(jnp.int32, (GLA_DK, GLA_TILE), 1)
    for t in range(tiles):
        cols = slice(t * GLA_TILE, (t + 1) * GLA_TILE)
        k_half, decay, v = [], [], []
        for b in range(batch):
            _, v_ref, _, kt_ref, glrt_ref = seq_refs[b]
            glr = glrt_ref[:, cols].astype(BF16)
            pre = jnp.dot(w2t_ref[...], glr, preferred_element_type=F32) + bg_ref[...]
            la = _log_sigmoid(pre) * (1.0 / GLA_GATE_NORMALIZER)
            hi, lo = _split_bf16(la)
            sums = (jnp.dot(hi, cm_ref[...], preferred_element_type=F32)
                    + jnp.dot(lo, cm_ref[...], preferred_element_type=F32))
            k_end = (kt_ref[:, cols] * jnp.exp(sums[:, :GLA_TILE])).astype(BF16)
            decay.append((jnp.exp(sums[:, GLA_TILE:2 * GLA_TILE]),
                          jnp.exp(sums[:, 2 * GLA_TILE:])))
            k_half.append((jnp.where(lane < CHUNK, k_end, jnp.zeros_like(k_end)),
                           jnp.where(lane >= CHUNK, k_end, jnp.zeros_like(k_end))))
            v.append(v_ref[cols, :].astype(BF16))
        for j in range(2):
            rows = slice(t * GLA_TILE + j * CHUNK, t * GLA_TILE + (j + 1) * CHUNK)
            for b in range(batch):
                q_ref, _, r_ref, _, _ = seq_refs[b]
                for h in range(GLA_HEADS):
                    kr = slice(h * GLA_HEAD_K, (h + 1) * GLA_HEAD_K)
                    vc = slice(h * GLA_HEAD_V, (h + 1) * GLA_HEAD_V)
                    upd = jnp.dot(k_half[b][j][kr, :], v[b][:, vc], preferred_element_type=F32)
                    s_ref[b, kr, vc] = decay[b][j][kr, :] * s_ref[b, kr, vc] + upd
                q = q_ref[rows, :].astype(BF16)
                o = jnp.dot(q, s_ref[b].astype(BF16), preferred_element_type=F32)
                o = o * (GLA_HEAD_K ** -0.5)
                gate = r_ref[rows, :]
                for h in range(GLA_HEADS):
                    vc = slice(h * GLA_HEAD_V, (h + 1) * GLA_HEAD_V)
                    oh = _rms(o[:, vc], gout_ref[...])
                    g = gate[:, vc]
                    o_ref[b, rows, vc] = (oh * (g * _sigmoid(g))).astype(o_ref.dtype)


def _gla(proj, kt, glrt, w2t, bg, cm, gout, batch, seq, tr=512):
    nt = seq // tr
    kernel = functools.partial(_gla_kernel, tiles=tr // GLA_TILE, batch=batch)
    seq_specs, seq_args = [], []
    for b in range(batch):
        row = lambda i, b=b: b * nt + i
        seq_specs += [
            pl.BlockSpec((tr, GLA_DK), lambda i, row=row: (row(i), COL_GQ // GLA_DK)),
            pl.BlockSpec((tr, GLA_DV), lambda i, row=row: (row(i), COL_GV // GLA_DV)),
            pl.BlockSpec((tr, GLA_DV), lambda i, row=row: (row(i), COL_GR // GLA_DV)),
            pl.BlockSpec((GLA_DK, tr), lambda i, row=row: (0, row(i))),
            pl.BlockSpec((LANES, tr), lambda i, row=row: (0, row(i))),
        ]
        seq_args += [proj, proj, proj, kt, glrt]
    out = pl.pallas_call(
        kernel,
        grid=(nt,),
        in_specs=seq_specs + [
            _const_spec((GLA_DK, LANES)),
            _const_spec((GLA_DK, LANES)),
            _const_spec((GLA_TILE, 3 * GLA_TILE)),
            _const_spec((1, GLA_HEAD_V)),
        ],
        out_specs=pl.BlockSpec((batch, tr, GLA_DV), lambda i: (0, i, 0)),
        out_shape=jax.ShapeDtypeStruct((batch, seq, GLA_DV), ACT_DTYPE),
        scratch_shapes=[pltpu.VMEM((batch, GLA_DK, GLA_DV), F32)],
        compiler_params=_cparams(("arbitrary",), 32),
        name="gla",
    )(*seq_args, w2t, bg, cm, gout)
    return out.reshape(batch * seq, GLA_DV)


SB_TQ = 256
SB_SUBS = 8
SB_DEAD_LOG2 = 160.0
SB_NO_BLOCK = 1e30


def _sb_kernel(q_ref, k_ref, v_ref, o_ref, acc_ref, c_ref):
    step = pl.program_id(2)
    tq = SB_TQ
    lane = lax.broadcasted_iota(jnp.int32, (tq, LANES), 1)
    row = lax.broadcasted_iota(jnp.int32, (tq, tq), 0)
    col = lax.broadcasted_iota(jnp.int32, (tq, tq), 1)
    from_s = jnp.where(row >= col, 1.0, 0.0).astype(BF16)
    from_s = jnp.concatenate([from_s, from_s], axis=0)
    past = col < row

    def q_heads(u):
        q = q_ref[u * tq:(u + 1) * tq, :].astype(BF16)
        zero = jnp.zeros_like(q)
        return (jnp.where(lane < SB_HEAD_DIM, q, zero), jnp.where(lane >= SB_HEAD_DIM, q, zero))

    def load(kb):
        start = pl.multiple_of(kb * tq, tq)
        return k_ref[pl.ds(start, tq), :].astype(BF16), v_ref[pl.ds(start, tq), :].astype(BF16)

    def neg_log2_keep(qh, kblk):
        y = lax.dot_general(qh, kblk, NT_DIMS, preferred_element_type=F32)
        return y, jnp.maximum(y, 0.0) + jnp.log2(1.0 + jnp.exp2(-jnp.abs(y)))

    def weights(y, p, carry):
        total = jnp.dot(jnp.concatenate(_split_bf16(p), axis=1), from_s, preferred_element_type=F32)
        if carry is not None:
            total = total + carry
        return jnp.exp2(y - total)

    def pv(w, vblk):
        return jnp.dot(w.astype(BF16), vblk, preferred_element_type=F32)

    first = step * SB_SUBS
    blocks = [load(jnp.maximum(first - 1, 0))] + [load(first + u) for u in range(SB_SUBS)]
    no_prev = jnp.where(first > 0, 0.0, SB_NO_BLOCK)
    chains = [(u, h) for u in range(SB_SUBS) for h in range(2)]
    qh = [q_heads(u) for u in range(SB_SUBS)]
    alive = []

    def scores(u, h):
        y, p = neg_log2_keep(qh[u][h], blocks[u + 1][0])
        return (y, jnp.where(past, p, 0.0)), neg_log2_keep(qh[u][h], blocks[u][0])

    def finish(u, h, diag, prev):
        yd, pd = diag
        yp, pp = prev
        cd = jnp.sum(pd, axis=-1, keepdims=True)
        wd = jnp.where(past, weights(yd, pd, None), 0.0)
        wp = weights(yp, pp, cd + no_prev if u == 0 else cd)
        acc_ref[u, h] = pv(wd, blocks[u + 1][1]) + pv(wp, blocks[u][1])
        c = cd + jnp.sum(pp, axis=-1, keepdims=True)
        c_ref[u, h] = jnp.broadcast_to(c, (tq, LANES))
        alive.append(jnp.min(c))

    pending = scores(*chains[0])
    for i, (u, h) in enumerate(chains):
        upcoming = scores(*chains[i + 1]) if i + 1 < len(chains) else None
        finish(u, h, *pending)
        pending = upcoming

    def cond(state):
        kb, go = state
        return jnp.logical_and(kb >= 0, go > 0)

    for u in range(SB_SUBS):
        def body(state, u=u):
            kb, _ = state
            kblk, vblk = load(kb)
            cmin = None
            for h in range(2):
                y, p = neg_log2_keep(qh[u][h], kblk)
                c = c_ref[u, h]
                w = weights(y, p, jnp.concatenate([c, c], axis=1))
                acc_ref[u, h] += pv(w, vblk)
                c = c + jnp.sum(p, axis=-1, keepdims=True)
                c_ref[u, h] = c
                m = jnp.min(c)
                cmin = m if cmin is None else jnp.minimum(cmin, m)
            return kb - 1, (cmin < SB_DEAD_LOG2).astype(jnp.int32)

        carry_min = jnp.minimum(alive[2 * u], alive[2 * u + 1])
        lax.while_loop(cond, body, (first + u - 2, (carry_min < SB_DEAD_LOG2).astype(jnp.int32)))
        o_ref[u * tq:(u + 1) * tq, :] = jnp.where(
            lane < SB_HEAD_DIM, acc_ref[u, 0], acc_ref[u, 1]).astype(o_ref.dtype)


def _sb_attention(qn, kn, vb, batch, seq):
    rows = SB_SUBS * SB_TQ
    pairs = SB_D // LANES
    q3, k3, v3 = (a.reshape(batch, seq, SB_D) for a in (qn, kn, vb))
    out = pl.pallas_call(
        _sb_kernel,
        grid=(batch, pairs, seq // rows),
        in_specs=[
            pl.BlockSpec((None, rows, LANES), lambda b, p, i: (b, i, p)),
            pl.BlockSpec((None, seq, LANES), lambda b, p, i: (b, 0, p)),
            pl.BlockSpec((None, seq, LANES), lambda b, p, i: (b, 0, p)),
        ],
        out_specs=pl.BlockSpec((None, rows, LANES), lambda b, p, i: (b, i, p)),
        out_shape=jax.ShapeDtypeStruct((batch, seq, SB_D), ACT_DTYPE),
        scratch_shapes=[pltpu.VMEM((SB_SUBS, 2, SB_TQ, LANES), F32),
                        pltpu.VMEM((SB_SUBS, 2, SB_TQ, LANES), F32)],
        compiler_params=_cparams(("parallel", "parallel", "arbitrary"), 32),
        name="sb_attention",
    )(q3, k3, v3)
    return out.reshape(batch * seq, SB_D)


FF_CHUNK = 256


def _ffn_tail(h1, g_ref, wg_ref, wu_ref, wd_ref, out_ref, acc_ref):
    hb = _rms(h1, g_ref[...]).astype(BF16)
    acc_ref[...] = h1
    for c in range(D_FF // FF_CHUNK):
        cols = slice(c * FF_CHUNK, (c + 1) * FF_CHUNK)
        a = jnp.dot(hb, wg_ref[:, cols].astype(BF16), preferred_element_type=F32)
        u = jnp.dot(hb, wu_ref[:, cols].astype(BF16), preferred_element_type=F32)
        act = (a * _sigmoid(a) * u).astype(BF16)
        acc_ref[...] += jnp.dot(act, wd_ref[cols, :].astype(BF16), preferred_element_type=F32)
    out_ref[...] = acc_ref[...]


def _outproj_ffn_kernel(h_ref, og_ref, os_ref, wo_ref, g_ref, wg_ref, wu_ref, wd_ref,
                        out_ref, acc_ref):
    mix = (jnp.dot(og_ref[...].astype(BF16), wo_ref[:GLA_DV, :], preferred_element_type=F32)
           + jnp.dot(os_ref[...].astype(BF16), wo_ref[GLA_DV:, :], preferred_element_type=F32))
    _ffn_tail(h_ref[...] + mix, g_ref, wg_ref, wu_ref, wd_ref, out_ref, acc_ref)


def _ffn_specs():
    return [
        _const_spec((1, D_MODEL)),
        _const_spec((D_MODEL, D_FF)),
        _const_spec((D_MODEL, D_FF)),
        _const_spec((D_FF, D_MODEL)),
    ]


def _outproj_ffn(h2d, o_gla, o_sb, wo, g, wg, wu, wd, layer, tm=512):
    m = h2d.shape[0]
    row = pl.BlockSpec((tm, D_MODEL), lambda i: (i, 0))
    half = pl.BlockSpec((tm, GLA_DV), lambda i: (i, 0))

    def stacked(rows, cols):
        return pl.BlockSpec((None, rows, cols), lambda i: (layer, 0, 0), pipeline_mode=pl.Buffered(1))

    return pl.pallas_call(
        _outproj_ffn_kernel,
        grid=(m // tm,),
        in_specs=[row, half, half, _const_spec((D_MODEL, D_MODEL)), _const_spec((1, D_MODEL)),
                  stacked(D_MODEL, D_FF), stacked(D_MODEL, D_FF), stacked(D_FF, D_MODEL)],
        out_specs=row,
        out_shape=jax.ShapeDtypeStruct((m, D_MODEL), F32),
        scratch_shapes=[pltpu.VMEM((tm, D_MODEL), F32)],
        compiler_params=_cparams(("parallel",), 56),
        name="outproj_ffn",
    )(h2d, o_gla, o_sb, wo, g, wg, wu, wd)


HALO = 32
CONV_ROWS = 128
CONV_COLS = 256


def _interleave(first, second):
    lists = (first, second)
    total = [sum(cost for cost, _ in lst) for lst in lists]
    done, idx = [0.0, 0.0], [0, 0]
    while idx[0] < len(first) or idx[1] < len(second):
        if idx[1] >= len(second):
            k = 0
        elif idx[0] >= len(first):
            k = 1
        else:
            k = 0 if done[0] / total[0] <= done[1] / total[1] else 1
        cost, thunk = lists[k][idx[k]]
        thunk()
        done[k] += cost
        idx[k] += 1


def _layer1_kernel(h_ref, g_ref, w1_ref, b1_ref, wdw_ref, bdw_ref, lng_ref, lnb_ref,
                   w2_ref, b2_ref, gf_ref, wg_ref, wu_ref, wd_ref, out_ref,
                   u_ref, y_ref, sh_ref, mid_ref, acc_ref, *, tiles_per_seq):
    s = pl.program_id(0)
    tc = h_ref.shape[0]

    @pl.when(s == 0)
    def _():
        mid_ref[...] = jnp.zeros_like(mid_ref)

    @pl.when(s % tiles_per_seq == 0)
    def _():
        u_ref[:HALO, :] = jnp.zeros((HALO, D_MODEL), F32)

    prev = mid_ref[(s + 1) % 2]
    hb = _rms(prev, gf_ref[...]).astype(BF16)
    acc_ref[...] = prev

    def ffn_chunk(c):
        cols = slice(c * FF_CHUNK, (c + 1) * FF_CHUNK)
        a = jnp.dot(hb, wg_ref[:, cols], preferred_element_type=F32)
        u = jnp.dot(hb, wu_ref[:, cols], preferred_element_type=F32)
        act = (a * _sigmoid(a) * u).astype(BF16)
        acc_ref[...] += jnp.dot(act, wd_ref[cols, :], preferred_element_type=F32)

    ffn_pieces = [(1270, functools.partial(ffn_chunk, c)) for c in range(D_FF // FF_CHUNK)]

    xb = _rms(h_ref[...], g_ref[...]).astype(BF16)
    shift = HALO - (CONV_WIDTH - 1)
    span = HALO + tc

    def glu(c):
        cols = slice(c * CONV_COLS, (c + 1) * CONV_COLS)
        gcols = slice(D_MODEL + c * CONV_COLS, D_MODEL + (c + 1) * CONV_COLS)
        lin = jnp.dot(xb, w1_ref[:, cols], preferred_element_type=F32) + b1_ref[:, cols]
        gate = jnp.dot(xb, w1_ref[:, gcols], preferred_element_type=F32) + b1_ref[:, gcols]
        u_ref[HALO:, cols] = lin * _sigmoid(gate)

    def shifted_copies(c):
        slab = u_ref[:, c * CONV_COLS:(c + 1) * CONV_COLS]
        for r in range(1, SUBLANES):
            sh_ref[r - 1] = pltpu.roll(slab, span - r, axis=0)

    def conv(c, rb):
        cols = slice(c * CONV_COLS, (c + 1) * CONV_COLS)
        r0 = rb * CONV_ROWS
        acc = jnp.broadcast_to(bdw_ref[:, cols], (CONV_ROWS, CONV_COLS))
        for j in range(CONV_WIDTH):
            a, r = divmod(shift + j, SUBLANES)
            rows = slice(r0 + SUBLANES * a, r0 + SUBLANES * a + CONV_ROWS)
            tap = u_ref[rows, cols] if r == 0 else sh_ref[r - 1, rows, :]
            acc = acc + wdw_ref[j:j + 1, cols] * tap
        y_ref[r0:r0 + CONV_ROWS, cols] = acc

    def carry_halo():
        u_ref[:HALO, :] = u_ref[tc:tc + HALO, :]

    def norm_project(rb):
        rows = slice(rb * CONV_ROWS, (rb + 1) * CONV_ROWS)
        y = y_ref[rows, :]
        mu = jnp.mean(y, axis=-1, keepdims=True)
        d = y - mu
        var = jnp.mean(d * d, axis=-1, keepdims=True)
        yn = d * lax.rsqrt(var + EPS) * lng_ref[...] + lnb_ref[...]
        act = (yn * _sigmoid(yn)).astype(BF16)
        mid_ref[s % 2, rows, :] = (h_ref[rows, :] + b2_ref[...]
                                   + jnp.dot(act, w2_ref[...], preferred_element_type=F32))

    conf_pieces = []
    for c in range(D_MODEL // CONV_COLS):
        conf_pieces.append((270, functools.partial(glu, c)))
        conf_pieces.append((800, functools.partial(shifted_copies, c)))
        conf_pieces += [(520, functools.partial(conv, c, rb)) for rb in range(tc // CONV_ROWS)]
    conf_pieces.append((10, carry_halo))
    conf_pieces += [(750, functools.partial(norm_project, rb)) for rb in range(tc // CONV_ROWS)]

    _interleave(conf_pieces, ffn_pieces)
    out_ref[...] = acc_ref[...]


def _layer1(h2d, g, w1, b1, wdw, bdw, lng, lnb, w2, b2, gf, wg, wu, wd, seq, tc=512):
    m = h2d.shape[0]
    tiles = m // tc
    kernel = functools.partial(_layer1_kernel, tiles_per_seq=seq // tc)
    return pl.pallas_call(
        kernel,
        grid=(tiles + 1,),
        in_specs=[
            pl.BlockSpec((tc, D_MODEL), lambda s: (jnp.minimum(s, tiles - 1), 0)),
            _const_spec((1, D_MODEL)),
            _const_spec((D_MODEL, 2 * D_MODEL)),
            _const_spec((1, 2 * D_MODEL)),
            _const_spec((HALO, D_MODEL)),
            _const_spec((1, D_MODEL)),
            _const_spec((1, D_MODEL)),
            _const_spec((1, D_MODEL)),
            _const_spec((D_MODEL, D_MODEL)),
            _const_spec((1, D_MODEL)),
        ] + _ffn_specs(),
        out_specs=pl.BlockSpec((tc, D_MODEL), lambda s: (jnp.maximum(s - 1, 0), 0)),
        out_shape=jax.ShapeDtypeStruct((m, D_MODEL), F32),
        scratch_shapes=[
            pltpu.VMEM((HALO + tc, D_MODEL), F32),
            pltpu.VMEM((tc, D_MODEL), F32),
            pltpu.VMEM((SUBLANES - 1, HALO + tc, CONV_COLS), F32),
            pltpu.VMEM((2, tc, D_MODEL), F32),
            pltpu.VMEM((tc, D_MODEL), F32),
        ],
        compiler_params=_cparams(("arbitrary",), 58),
        name="layer1",
    )(h2d, g, w1, b1, wdw, bdw, lng, lnb, w2, b2, gf, wg, wu, wd)


def kernel(x, mix_norm, ffn_norm, hy_w_in, hy_w_gate2, hy_b_gate, hy_gla_norm, hy_sb_q_norm,
           hy_sb_k_norm, hy_w_out, cv_w_pw1, cv_b_pw1, cv_w_dw, cv_b_dw, cv_ln_g, cv_ln_b,
           cv_w_pw2, cv_b_pw2, ffn_w_gate, ffn_w_up, ffn_w_down):
    batch, seq, _ = x.shape
    m = batch * seq
    h = x.reshape(m, D_MODEL)

    w_in = hy_w_in[0]
    cuts = np.cumsum([0, GLA_DK, GLA_DK, GLA_DV, GLA_DV, GLA_GATE_RANK, SB_D, SB_D, SB_D])
    w_gq, w_gk, w_gv, w_gr, w_glr, w_sq, w_sk, w_sv = (
        w_in[:, cuts[i]:cuts[i + 1]] for i in range(8))
    w_row = jnp.concatenate([w_gv, w_gr, w_gq], axis=1).astype(BF16)
    w_sb = w_in[:, cuts[5]:cuts[8]].astype(BF16)
    w_kt = w_gk.T.astype(BF16)
    w_gt = jnp.pad(w_glr.T, ((0, LANES - GLA_GATE_RANK), (0, 0))).astype(BF16)
    gq2 = jnp.tile(hy_sb_q_norm[0], 2)[None, :]
    gk2 = jnp.tile(hy_sb_k_norm[0], 2)[None, :]
    proj, qn, kn, vb, kt, glrt = _inproj(h, mix_norm[0][None, :], w_row, w_sb, w_kt, w_gt, gq2, gk2)

    w2t = jnp.pad(hy_w_gate2[0].T, ((0, 0), (0, LANES - GLA_GATE_RANK))).astype(BF16)
    bg = jnp.broadcast_to(hy_b_gate[0][:, None], (GLA_DK, LANES)).astype(F32)
    cm = jnp.asarray(_gla_cumsum_matrix(), dtype=BF16)
    o_gla = _gla(proj, kt, glrt, w2t, bg, cm, hy_gla_norm[0][None, :], batch, seq)

    o_sb = _sb_attention(qn, kn, vb, batch, seq)

    h = _outproj_ffn(h, o_gla, o_sb, hy_w_out[0].astype(BF16), ffn_norm[0][None, :],
                     ffn_w_gate, ffn_w_up, ffn_w_down, layer=0)

    wdw = jnp.pad(cv_w_dw[0], ((0, HALO - CONV_WIDTH), (0, 0)))
    h = _layer1(h, mix_norm[1][None, :], cv_w_pw1[0].astype(BF16), cv_b_pw1[0][None, :],
                wdw, cv_b_dw[0][None, :], cv_ln_g[0][None, :], cv_ln_b[0][None, :],
                cv_w_pw2[0].astype(BF16), cv_b_pw2[0][None, :], ffn_norm[1][None, :],
                ffn_w_gate[1].astype(BF16), ffn_w_up[1].astype(BF16), ffn_w_down[1].astype(BF16), seq)
    return h.reshape(batch, seq, D_MODEL)
```
